```python
import math
import jax, jax.numpy as jnp
from jax import lax
import numpy as np

D_MODEL = 1024
BATCH = 4
SEQ = 8192
DEPTH = 4

GRID_W = 64
CTX_LEN = 256
EPS = 1e-6
N_HEADS = 8
HEAD_DIM = 64
V_DIM = 2 * HEAD_DIM
QK_W = N_HEADS * 2 * HEAD_DIM
ATTN_W = N_HEADS * V_DIM
ROPE_HALF = HEAD_DIM // 2
ROPE_BASE = 10000.0
Q_BLOCK = 128
POOL_SIZES = (2, 4, 8, 16)
N_POOL_GROUPS = 4
POOL_W = D_MODEL
POOL_G = POOL_W // N_POOL_GROUPS
CONV_W = D_MODEL
CONV_K = 3
N_BRANCH = 3
N_MOD = 6
N_KEYS = 128
N_EXPERTS = N_KEYS * N_KEYS
PEER_HEADS = 8
PEER_TOPK = 16
PEER_QDIM = 256
PEER_HALF = PEER_QDIM // 2
PEER_CHUNK = 128
Q_OFF = 0
K_OFF = Q_OFF + QK_W
V_OFF = K_OFF + QK_W
POOL_OFF = V_OFF + ATTN_W
CIN_OFF = POOL_OFF + POOL_W
CB_OFF = CIN_OFF + CONV_W
CC_OFF = CB_OFF + CONV_W
GATE_OFF = CC_OFF + CONV_W
IN_W = GATE_OFF + N_BRANCH * D_MODEL

kernel_name = "hybrid_diffattn_pool_shortconv_peer_dit"


def rms_norm(x, g):
    xf = x.astype(jnp.float32)
    y = xf * lax.rsqrt(jnp.mean(xf * xf, axis=-1, keepdims=True) + EPS)
    return (y * g.astype(jnp.float32)).astype(x.dtype)


def modulate(xn, shift, scale):
    return xn * (1 + scale) + shift


def axial_rope(n):
    n_rows = n // GRID_W
    rows = jnp.repeat(jnp.arange(n_rows, dtype=jnp.float32), GRID_W)
    cols = jnp.tile(jnp.arange(GRID_W, dtype=jnp.float32), n_rows)
    n_freq = ROPE_HALF // 2
    inv = ROPE_BASE ** (-jnp.arange(n_freq, dtype=jnp.float32) / n_freq)
    ang = jnp.concatenate([rows[:, None] * inv, cols[:, None] * inv], axis=-1)
    return jnp.cos(ang), jnp.sin(ang)


def apply_rope(x, cos, sin):
    cos = cos[None, :, None, None, :]
    sin = sin[None, :, None, None, :]
    x1, x2 = x[..., :ROPE_HALF], x[..., ROPE_HALF:]
    return jnp.concatenate([x1 * cos - x2 * sin, x2 * cos + x1 * sin], axis=-1).astype(x.dtype)


def qk_heads(p, g):
    b, l = p.shape[:2]
    return rms_norm(p.reshape(b, l, N_HEADS, 2, HEAD_DIM), g)


def diff_lambda(lv, lam_init):
    lvf = lv.astype(jnp.float32)
    return jnp.exp(jnp.sum(lvf[0] * lvf[1])) - jnp.exp(jnp.sum(lvf[2] * lvf[3])) + lam_init


def diff_attn_core(q, k, v, lam):
    s = jnp.einsum('bqhid,bkhid->bhiqk', q, k).astype(jnp.float32) * (HEAD_DIM ** -0.5)
    pr = jax.nn.softmax(s, axis=-1)
    a = pr[:, :, 0] - lam * pr[:, :, 1]
    return jnp.einsum('bhqk,bkhe->bqhe', a.astype(v.dtype), v)


def blocked_diff_attn(q, k, v, lam):
    b, l = q.shape[:2]
    nb = l // Q_BLOCK
    qb = q.reshape(b, nb, Q_BLOCK, N_HEADS, 2, HEAD_DIM).swapaxes(0, 1)
    out = lax.map(lambda qq: diff_attn_core(qq, k, v, lam), qb)
    return out.swapaxes(0, 1).reshape(b, l, N_HEADS, V_DIM)


def attn_out_norm(o, g, lam_init):
    b, l = o.shape[:2]
    return (rms_norm(o, g) * (1.0 - lam_init)).reshape(b, l, ATTN_W)


def centred_pool_minus_self(u, w):
    l = u.shape[1]
    lo = w // 2
    hi = w - 1 - lo
    uf = u.astype(jnp.float32)
    cs = jnp.pad(jnp.cumsum(uf, axis=1), ((0, 0), (1, 0), (0, 0)))
    t = jnp.arange(l)
    start = jnp.clip(t - lo, 0, l)
    end = jnp.clip(t + hi + 1, 0, l)
    cnt = (end - start).astype(jnp.float32)
    mean = (cs[:, end] - cs[:, start]) / cnt[None, :, None]
    return (mean - uf).astype(u.dtype)


def pool_branch(u, pool_w, pool_scale):
    b, l, _ = u.shape
    ug = u.reshape(b, l, N_POOL_GROUPS, POOL_G)
    pooled = jnp.stack([centred_pool_minus_self(ug[:, :, i], w) for i, w in enumerate(POOL_SIZES)], axis=2)
    y = jnp.einsum('blgc,gcd->blgd', pooled, pool_w).reshape(b, l, POOL_W)
    return y * pool_scale


def depthwise_conv(u, w):
    pad = CONV_K // 2
    return lax.conv_general_dilated(u, w[:, None, :], window_strides=(1,), padding=((pad, pad),),
                                    dimension_numbers=('NWC', 'WIO', 'NWC'),
                                    feature_group_count=u.shape[-1])


def conv_branch(h, b_gate, c_gate, conv_w, conv_out_w):
    return (b_gate * depthwise_conv(c_gate * h, conv_w)) @ conv_out_w


def mixer_output(o, p, attn_g, lam_init, pool_w, pool_scale, conv_w, conv_out_w, w_out):
    b, l = o.shape[:2]
    attn = attn_out_norm(o, attn_g, lam_init)
    pool = pool_branch(p[..., POOL_OFF:CIN_OFF], pool_w, pool_scale)
    conv = conv_branch(p[..., CIN_OFF:CB_OFF], p[..., CB_OFF:CC_OFF], p[..., CC_OFF:GATE_OFF], conv_w, conv_out_w)
    g = jax.nn.sigmoid(p[..., GATE_OFF:]).reshape(b, l, N_BRANCH, D_MODEL)
    merged = g[:, :, 0] * attn + g[:, :, 1] * pool + g[:, :, 2] * conv
    return merged @ w_out


def peer_ffn(x, w_q, sub_keys, u_emb, v_emb):
    b, l, d = x.shape
    t_all = b * l
    xb = x.reshape(t_all // PEER_CHUNK, PEER_CHUNK, d)

    def chunk(xc):
        q = (xc @ w_q).reshape(PEER_CHUNK, PEER_HEADS, 2, PEER_HALF)
        s = jnp.einsum('thid,hikd->thik', q, sub_keys).astype(jnp.float32)
        s1, i1 = lax.top_k(s[:, :, 0], PEER_TOPK)
        s2, i2 = lax.top_k(s[:, :, 1], PEER_TOPK)
        cand = (s1[..., :, None] + s2[..., None, :]).reshape(PEER_CHUNK, PEER_HEADS, PEER_TOPK * PEER_TOPK)
        cidx = (i1[..., :, None] * N_KEYS + i2[..., None, :]).reshape(PEER_CHUNK, PEER_HEADS, PEER_TOPK * PEER_TOPK)
        top_s, pos = lax.top_k(cand, PEER_TOPK)
        eidx = jnp.take_along_axis(cidx, pos, axis=-1)
        gate = jax.nn.softmax(top_s, axis=-1)
        u = u_emb[eidx]
        hid = jax.nn.gelu(jnp.einsum('td,thkd->thk', xc, u))
        wgt = (gate * hid.astype(jnp.float32)).astype(xc.dtype)
        return jnp.einsum('thk,thkd->td', wgt, v_emb[eidx])

    return lax.map(chunk, xb).reshape(b, l, d)


def setup_inputs(seed: int = 0) -> dict:
    key = jax.random.key(seed)
    ks = jax.random.split(key, 22)
    f32 = jnp.float32
    L = DEPTH
    D = D_MODEL

    def nrm(k, shape, scale):
        return jax.random.normal(k, shape, f32) * scale

    def gain(k, shape):
        return 1.0 + 0.05 * jax.random.normal(k, shape, f32)

    return {
        "x": nrm(ks[0], (BATCH, SEQ, D), 1.0),
        "c": nrm(ks[1], (BATCH, D), 1.0),
        "ctx": nrm(ks[2], (BATCH, CTX_LEN, D), 1.0),
        "c_ctx": nrm(ks[3], (D,), 1.0),
        "norm1_g": gain(ks[4], (L, D)),
        "norm2_g": gain(ks[5], (L, D)),
        "w_mod": nrm(ks[6], (L, D, N_MOD * D), 0.5 * D ** -0.5),
        "b_mod": nrm(ks[7], (L, N_MOD * D), 0.02),
        "w_in": nrm(ks[8], (L, D, IN_W), D ** -0.5),
        "q_norm_g": gain(ks[9], (L, HEAD_DIM)),
        "k_norm_g": gain(ks[10], (L, HEAD_DIM)),
        "lam_vecs": nrm(ks[11], (L, 4, HEAD_DIM), 0.1),
        "attn_norm_g": gain(ks[12], (L, V_DIM)),
        "pool_w": nrm(ks[13], (L, N_POOL_GROUPS, POOL_G, POOL_G), POOL_G ** -0.5),
        "pool_scale": gain(ks[14], (L, POOL_W)),
        "conv_w": nrm(ks[15], (L, CONV_K, CONV_W), CONV_K ** -0.5),
        "conv_out_w": nrm(ks[16], (L, CONV_W, D), CONV_W ** -0.5),
        "w_out": nrm(ks[17], (L, D, D), D ** -0.5),
        "peer_wq": nrm(ks[18], (L, D, PEER_HEADS * PEER_QDIM), D ** -0.5),
        "peer_keys": nrm(ks[19], (L, PEER_HEADS, 2, N_KEYS, PEER_HALF), PEER_HALF ** -0.5),
        "peer_u": nrm(ks[20], (L, N_EXPERTS, D), D ** -0.5),
        "peer_v": nrm(ks[21], (L, N_EXPERTS, D), PEER_HEADS ** -0.5),
    }


def reference(x, c, ctx, c_ctx, norm1_g, norm2_g, w_mod, b_mod, w_in, q_norm_g, k_norm_g,
              lam_vecs, attn_norm_g, pool_w, pool_scale, conv_w, conv_out_w, w_out,
              peer_wq, peer_keys, peer_u, peer_v):
    b, s, d = x.shape
    n_ctx = ctx.shape[1]
    cos, sin = axial_rope(s)
    h, hc = x, ctx
    for l in range(DEPTH):
        last = l == DEPTH - 1
        lam_init = 0.8 - 0.6 * math.exp(-0.3 * l)
        mod = (jax.nn.silu(c) @ w_mod[l] + b_mod[l]).reshape(b, N_MOD, 1, d)
        mod_c = (jax.nn.silu(c_ctx) @ w_mod[l] + b_mod[l]).reshape(N_MOD, 1, 1, d)
        lam = diff_lambda(lam_vecs[l], lam_init)
        wl = w_in[l]

        xn = modulate(rms_norm(h, norm1_g[l]), mod[:, 0], mod[:, 1])
        cn = modulate(rms_norm(hc, norm1_g[l]), mod_c[0], mod_c[1])
        if last:
            pkv = cn @ wl[:, K_OFF:POOL_OFF]
            kc_raw, vc_raw = pkv[..., :QK_W], pkv[..., QK_W:]
        else:
            pc = cn @ wl
            kc_raw, vc_raw = pc[..., K_OFF:V_OFF], pc[..., V_OFF:POOL_OFF]
        kc = qk_heads(kc_raw, k_norm_g[l])
        vc = vc_raw.reshape(b, n_ctx, N_HEADS, V_DIM)

        p = xn @ wl
        q = apply_rope(qk_heads(p[..., Q_OFF:K_OFF], q_norm_g[l]), cos, sin)
        k = apply_rope(qk_heads(p[..., K_OFF:V_OFF], k_norm_g[l]), cos, sin)
        v = p[..., V_OFF:POOL_OFF].reshape(b, s, N_HEADS, V_DIM)
        o = blocked_diff_attn(q, jnp.concatenate([k, kc], axis=1), jnp.concatenate([v, vc], axis=1), lam)
        mix = mixer_output(o, p, attn_norm_g[l], lam_init, pool_w[l], pool_scale[l],
                           conv_w[l], conv_out_w[l], w_out[l])

        if not last:
            qc = qk_heads(pc[..., Q_OFF:K_OFF], q_norm_g[l])
            oc = diff_attn_core(qc, kc, vc, lam)
            mix_c = mixer_output(oc, pc, attn_norm_g[l], lam_init, pool_w[l], pool_scale[l],
                                 conv_w[l], conv_out_w[l], w_out[l])
            hc = hc + mod_c[2] * mix_c
        h = h + mod[:, 2] * mix

        xn2 = modulate(rms_norm(h, norm2_g[l]), mod[:, 3], mod[:, 4])
        h = h + mod[:, 5] * peer_ffn(xn2, peer_wq[l], peer_keys[l], peer_u[l], peer_v[l])
        if not last:
            cn2 = modulate(rms_norm(hc, norm2_g[l]), mod_c[3], mod_c[4])
            hc = hc + mod_c[5] * peer_ffn(cn2, peer_wq[l], peer_keys[l], peer_u[l], peer_v[l])
    return h
```

```python
import functools
import math

import jax
import jax.numpy as jnp
from jax import lax
from jax.experimental import pallas as pl
from jax.experimental.pallas import tpu as pltpu

F32 = jnp.float32
BF16 = jnp.bfloat16

LANES = 128
SUBLANES = 8
VMEM_LIMIT_BYTES = 56 * 1024 * 1024

EPS = 1e-6
N_HEADS = 8
HEAD_DIM = 64
V_DIM = 2 * HEAD_DIM
ROPE_HALF = HEAD_DIM // 2
ROPE_BASE = 10000.0
GRID_W = 64
POOL_SIZES = (2, 4, 8, 16)
POOL_HALO = 8
N_MOD = 6
N_KEYS = 128
PEER_HEADS = 8
PEER_TOPK = 16
NEG_BIG = -1e30


def _cparams(semantics):
    return pltpu.CompilerParams(dimension_semantics=semantics, vmem_limit_bytes=VMEM_LIMIT_BYTES)


def _tile(n, prefs):
    for t in prefs:
        if n % t == 0:
            return t
    raise ValueError(f"no tile among {prefs} divides {n}")


def _dot(a, b):
    return jnp.dot(a, b, preferred_element_type=F32)


def _dot_nt(a, b):
    return lax.dot_general(a, b, (((1,), (1,)), ((), ())), preferred_element_type=F32)


def _mod_body(c_ref, w_ref, b_ref, o_ref):
    c = c_ref[...]
    s = c * jax.nn.sigmoid(c)
    o_ref[0] = _dot(s.astype(BF16), w_ref[0].astype(BF16)) + b_ref[0]


def modulation(cc, w_mod, b_mod):
    depth, d, n = w_mod.shape
    tn = _tile(n, (1536, 1024, 512))
    return pl.pallas_call(
        _mod_body,
        grid=(depth, n // tn),
        in_specs=[
            pl.BlockSpec((SUBLANES, d), lambda l, j: (0, 0)),
            pl.BlockSpec((1, d, tn), lambda l, j: (l, 0, j)),
            pl.BlockSpec((1, 1, tn), lambda l, j: (l, 0, j)),
        ],
        out_specs=pl.BlockSpec((1, SUBLANES, tn), lambda l, j: (l, 0, j)),
        out_shape=jax.ShapeDtypeStruct((depth, SUBLANES, n), F32),
        compiler_params=_cparams(("arbitrary", "arbitrary")),
        name="modulation",
    )(cc, w_mod, b_mod.reshape(depth, 1, n))


def _norm_matmul_body(h_ref, g_ref, sh_ref, sc_ref, w_ref, o_ref, *rest, emit_xn):
    xn_scr = rest[-1]

    @pl.when(pl.program_id(1) == 0)
    def _():
        x = h_ref[...]
        ms = jnp.mean(x * x, axis=-1, keepdims=True)
        y = x * lax.rsqrt(ms + EPS) * g_ref[...]
        xn = y * (1.0 + sc_ref[0, 0]) + sh_ref[0, 0]
        xn_scr[...] = xn.astype(BF16)
        if emit_xn:
            rest[0][...] = xn_scr[...]

    o_ref[...] = _dot(xn_scr[...], w_ref[...]).astype(o_ref.dtype)


def norm_matmul(h2, seq_len, gain, mod, shift_idx, scale_idx, w, out_dtype, emit_xn=False):
    t_all, d = h2.shape
    n = w.shape[1]
    tt = _tile(seq_len, (512, 256, 128))
    tn = _tile(n, (2048, 1024))
    out_shape = [jax.ShapeDtypeStruct((t_all, n), out_dtype)]
    out_specs = [pl.BlockSpec((tt, tn), lambda i, j: (i, j))]
    if emit_xn:
        out_shape.append(jax.ShapeDtypeStruct((t_all, d), BF16))
        out_specs.append(pl.BlockSpec((tt, d), lambda i, j: (i, 0)))
    res = pl.pallas_call(
        functools.partial(_norm_matmul_body, emit_xn=emit_xn),
        grid=(t_all // tt, n // tn),
        in_specs=[
            pl.BlockSpec((tt, d), lambda i, j: (i, 0)),
            pl.BlockSpec((1, d), lambda i, j: (0, 0)),
            pl.BlockSpec((1, 1, 1, d), lambda i, j: ((i * tt) // seq_len, shift_idx, 0, 0)),
            pl.BlockSpec((1, 1, 1, d), lambda i, j: ((i * tt) // seq_len, scale_idx, 0, 0)),
            pl.BlockSpec((d, tn), lambda i, j: (0, j)),
        ],
        out_specs=out_specs,
        out_shape=out_shape,
        scratch_shapes=[pltpu.VMEM((tt, d), BF16)],
        compiler_params=_cparams(("arbitrary", "arbitrary")),
        name="norm_matmul",
    )(h2, gain.reshape(1, d), mod, mod, w)
    return res if emit_xn else res[0]


def _qkv_post_body(pq_ref, pk_ref, pv_ref, cos_ref, sin_ref, qg_ref, kg_ref, bd_ref,
                   q_ref, k_ref, vt_ref, *, rope):
    lane = lax.broadcasted_iota(jnp.int32, (pq_ref.shape[1], LANES), 1)
    first_half = (lane % HEAD_DIM) < ROPE_HALF

    def norm_rope(x, g, scale):
        ms = _dot((x * x).astype(BF16), bd_ref[...]) * (1.0 / HEAD_DIM)
        y = x * lax.rsqrt(ms + EPS) * g
        if scale != 1.0:
            y = y * scale
        if not rope:
            return y
        cos = cos_ref[...]
        sin = sin_ref[...]
        parts = []
        for hd in range(y.shape[1] // LANES):
            yh = y[:, hd * LANES:(hd + 1) * LANES]
            partner = jnp.where(first_half, pltpu.roll(yh, LANES - ROPE_HALF, 1), pltpu.roll(yh, ROPE_HALF, 1))
            parts.append(yh * cos + partner * sin)
        return jnp.concatenate(parts, axis=1)

    q_ref[0] = norm_rope(pq_ref[0].astype(F32), qg_ref[...], HEAD_DIM ** -0.5).astype(BF16)
    k_ref[0] = norm_rope(pk_ref[0].astype(F32), kg_ref[...], 1.0).astype(BF16)
    vt_ref[0] = pv_ref[0].astype(F32).T.astype(BF16)


def qkv_post(p3, cos_t, sin_t, q_gain, k_gain, block_diag, rope):
    b, l, _ = p3.shape
    w = N_HEADS * V_DIM
    tt = _tile(l, (256, 128))
    col = lambda c: pl.BlockSpec((1, tt, w), lambda i, j, c=c: (i, j, c))
    return pl.pallas_call(
        functools.partial(_qkv_post_body, rope=rope),
        grid=(b, l // tt),
        in_specs=[
            col(0), col(1), col(2),
            pl.BlockSpec((tt, LANES), lambda i, j: (j, 0)),
            pl.BlockSpec((tt, LANES), lambda i, j: (j, 0)),
            pl.BlockSpec((1, w), lambda i, j: (0, 0)),
            pl.BlockSpec((1, w), lambda i, j: (0, 0)),
            pl.BlockSpec((w, w), lambda i, j: (0, 0)),
        ],
        out_specs=[
            pl.BlockSpec((1, tt, w), lambda i, j: (i, j, 0)),
            pl.BlockSpec((1, tt, w), lambda i, j: (i, j, 0)),
            pl.BlockSpec((1, w, tt), lambda i, j: (i, 0, j)),
        ],
        out_shape=[
            jax.ShapeDtypeStruct((b, l, w), BF16),
            jax.ShapeDtypeStruct((b, l, w), BF16),
            jax.ShapeDtypeStruct((b, w, l), BF16),
        ],
        compiler_params=_cparams(("arbitrary", "arbitrary")),
        name="qkv_post",
    )(p3, p3, p3, cos_t, sin_t, q_gain, k_gain, block_diag)


def _attention_body(lv_ref, g_ref, q_ref, k_ref, vt_ref, o_ref, m_scr, l_scr, acc_scr,
                    *, tq, tk, lk, lam_init):
    lane = lax.broadcasted_iota(jnp.int32, (tq, LANES), 1)
    q = q_ref[0].astype(F32)
    q2 = jnp.concatenate([jnp.where(lane < HEAD_DIM, q, 0.0), jnp.where(lane >= HEAD_DIM, q, 0.0)], axis=0)
    q2 = q2.astype(BF16)

    m_scr[...] = jnp.full(m_scr.shape, NEG_BIG, F32)
    l_scr[...] = jnp.zeros(l_scr.shape, F32)
    acc_scr[...] = jnp.zeros(acc_scr.shape, F32)

    def block(start, size):
        kb = k_ref[0, pl.ds(start, size), :]
        s = _dot_nt(kb, q2)
        m_old = m_scr[...]
        m_new = jnp.maximum(m_old, jnp.max(s, axis=0, keepdims=True))
        alpha = jnp.exp(m_old - m_new)
        p = jnp.exp(s - m_new)
        l_scr[...] = alpha * l_scr[...] + jnp.sum(p, axis=0, keepdims=True)
        m_scr[...] = m_new
        vb = vt_ref[0, :, pl.ds(start, size)]
        acc_scr[...] = acc_scr[...] * alpha + _dot(vb, p.astype(BF16))

    n_full = lk // tk
    rem = lk - n_full * tk
    if n_full > 0:
        def step(j, carry):
            block(pl.multiple_of(j * tk, tk), tk)
            return carry
        lax.fori_loop(0, n_full, step, 0)
    if rem > 0:
        block(n_full * tk, rem)

    lv = lv_ref[...]
    lam = (jnp.exp(jnp.sum(lv[0:1] * lv[1:2], axis=1, keepdims=True))
           - jnp.exp(jnp.sum(lv[2:3] * lv[3:4], axis=1, keepdims=True)) + lam_init)
    acc = acc_scr[...]
    l = l_scr[...]
    o = acc[:, :tq] / l[:, :tq] - lam * (acc[:, tq:] / l[:, tq:])
    ms = jnp.mean(o * o, axis=0, keepdims=True)
    y = o * lax.rsqrt(ms + EPS) * g_ref[...] * (1.0 - lam_init)
    o_ref[0] = y.T


def attention(q, k, vt, lam_vecs, out_gain, lam_init):
    b, lq, w = q.shape
    lk = k.shape[1]
    tq = _tile(lq, (256, 128))
    tk = min(512, lk)
    return pl.pallas_call(
        functools.partial(_attention_body, tq=tq, tk=tk, lk=lk, lam_init=lam_init),
        grid=(b, N_HEADS, lq // tq),
        in_specs=[
            pl.BlockSpec((4, HEAD_DIM), lambda i, h, j: (0, 0)),
            pl.BlockSpec((V_DIM, 1), lambda i, h, j: (0, 0)),
            pl.BlockSpec((1, tq, V_DIM), lambda i, h, j: (i, j, h)),
            pl.BlockSpec((1, lk, V_DIM), lambda i, h, j: (i, 0, h)),
            pl.BlockSpec((1, V_DIM, lk), lambda i, h, j: (i, h, 0)),
        ],
        out_specs=pl.BlockSpec((1, tq, V_DIM), lambda i, h, j: (i, j, h)),
        out_shape=jax.ShapeDtypeStruct((b, lq, w), F32),
        scratch_shapes=[
            pltpu.VMEM((1, 2 * tq), F32),
            pltpu.VMEM((1, 2 * tq), F32),
            pltpu.VMEM((V_DIM, 2 * tq), F32),
        ],
        compiler_params=_cparams(("arbitrary", "arbitrary", "arbitrary")),
        name="attention",
    )(lam_vecs, out_gain.reshape(V_DIM, 1), q, k, vt)


def _mixer_body(h_ref, a_ref, u_ref, up_ref, un_ref, ch_ref, chp_ref, chn_ref, cb_ref,
                cc_ref, ccp_ref, ccn_ref, g0_ref, g1_ref, g2_ref,
                pw_ref, ps_ref, cw_ref, cow_ref, wo_ref, m2_ref, o_ref, *, tt, seq_len):
    t0 = pl.program_id(1) * tt
    rows = tt + 2 * POOL_HALO
    d = h_ref.shape[2]
    pos = lax.broadcasted_iota(jnp.int32, (rows, d), 0) + (t0 - POOL_HALO)
    in_seq = (pos >= 0) & (pos < seq_len)

    def extended(prev_ref, cur_ref, next_ref):
        e = jnp.concatenate([prev_ref[0].astype(F32), cur_ref[0].astype(F32), next_ref[0].astype(F32)], axis=0)
        return jnp.where(in_seq, e, 0.0)

    ue = extended(up_ref, u_ref, un_ref)
    u = ue[POOL_HALO:POOL_HALO + tt]
    gw = ue.shape[1] // len(POOL_SIZES)
    tpos = lax.broadcasted_iota(jnp.int32, (tt, gw), 0) + t0
    pooled = []
    for gi, wsz in enumerate(POOL_SIZES):
        lo = wsz // 2
        hi = wsz - 1 - lo
        e = ue[:, gi * gw:(gi + 1) * gw]
        acc = e
        span = 1
        while span < wsz:
            acc = acc + pltpu.roll(acc, rows - span, 0)
            span *= 2
        win = pltpu.roll(acc, lo, 0)[POOL_HALO:POOL_HALO + tt]
        cnt = (jnp.minimum(tpos + hi + 1, seq_len) - jnp.maximum(tpos - lo, 0)).astype(F32)
        pooled.append(win / cnt - u[:, gi * gw:(gi + 1) * gw])
    pool = jnp.concatenate(
        [_dot(pooled[gi].astype(BF16), pw_ref[gi]) for gi in range(len(POOL_SIZES))], axis=1) * ps_ref[...]

    che = extended(chp_ref, ch_ref, chn_ref) * extended(ccp_ref, cc_ref, ccn_ref)
    cw = cw_ref[...]
    conv = (cw[0:1] * pltpu.roll(che, 1, 0)[POOL_HALO:POOL_HALO + tt]
            + cw[1:2] * che[POOL_HALO:POOL_HALO + tt]
            + cw[2:3] * pltpu.roll(che, rows - 1, 0)[POOL_HALO:POOL_HALO + tt])
    conv = _dot((cb_ref[0].astype(F32) * conv).astype(BF16), cow_ref[...])

    merged = (jax.nn.sigmoid(g0_ref[0].astype(F32)) * a_ref[0]
              + jax.nn.sigmoid(g1_ref[0].astype(F32)) * pool
              + jax.nn.sigmoid(g2_ref[0].astype(F32)) * conv)
    mix = _dot(merged.astype(BF16), wo_ref[...])
    o_ref[0] = h_ref[0] + m2_ref[0, 0] * mix


def mixer(h3, attn, p3, pool_w, pool_scale, conv_w, conv_out_w, w_out, mod):
    b, l, d = h3.shape
    tt = _tile(l, (256, 128))
    hb = tt // POOL_HALO
    n_hb = l // POOL_HALO
    cur = lambda c: pl.BlockSpec((1, tt, d), lambda i, j, c=c: (i, j, c))
    prev = lambda c: pl.BlockSpec((1, POOL_HALO, d), lambda i, j, c=c: (i, jnp.maximum(j * hb - 1, 0), c))
    nxt = lambda c: pl.BlockSpec((1, POOL_HALO, d), lambda i, j, c=c: (i, jnp.minimum((j + 1) * hb, n_hb - 1), c))
    full = lambda shape: pl.BlockSpec(shape, lambda i, j: (0,) * len(shape))
    return pl.pallas_call(
        functools.partial(_mixer_body, tt=tt, seq_len=l),
        grid=(b, l // tt),
        in_specs=[
            cur(0), cur(0),
            cur(3), prev(3), nxt(3),
            cur(4), prev(4), nxt(4), cur(5),
            cur(6), prev(6), nxt(6),
            cur(7), cur(8), cur(9),
            full(pool_w.shape), full((1, d)), full(conv_w.shape), full((d, d)), full((d, d)),
            pl.BlockSpec((1, 1, 1, d), lambda i, j: (i, 2, 0, 0)),
        ],
        out_specs=pl.BlockSpec((1, tt, d), lambda i, j: (i, j, 0)),
        out_shape=jax.ShapeDtypeStruct((b, l, d), F32),
        compiler_params=_cparams(("arbitrary", "arbitrary")),
        name="mixer",
    )(h3, attn, p3, p3, p3, p3, p3, p3, p3, p3, p3, p3, p3, p3, p3,
      pool_w, pool_scale.reshape(1, d), conv_w, conv_out_w, w_out, mod)


def _extract_top(blocks, index_blocks, n_take):
    work = list(blocks)
    order = [jnp.full(b.shape, float(n_take), F32) for b in blocks]
    taken = []
    big = float(1 << 20)
    for a in range(n_take):
        m = work[0]
        for w in work[1:]:
            m = jnp.maximum(m, w)
        m = jnp.max(m, axis=0, keepdims=True)
        cand = [jnp.where(w == m, ib, big) for w, ib in zip(work, index_blocks)]
        first = cand[0]
        for c in cand[1:]:
            first = jnp.minimum(first, c)
        first = jnp.min(first, axis=0, keepdims=True)
        hit = [ib == first for ib in index_blocks]
        work = [jnp.where(ht, NEG_BIG * 2.0, w) for ht, w in zip(hit, work)]
        order = [jnp.where(ht, float(a), o) for ht, o in zip(hit, order)]
        taken.append(m)
    return taken, order


def _peer_route_body(q_ref, keys_ref, th_ref, e1_ref, rk_ref, e2_ref, *, tt):
    row = lax.broadcasted_iota(jnp.int32, (SUBLANES, tt), 0).astype(F32)
    n_blk = N_KEYS // SUBLANES
    key_index = [row + float(SUBLANES * i) for i in range(n_blk)]
    k16 = PEER_TOPK
    half_blocks = k16 // SUBLANES

    def stack_rows(vals):
        out = jnp.zeros((SUBLANES, tt), F32)
        for i, v in enumerate(vals):
            out = jnp.where(row == float(i), v, out)
        return out

    def head(hd, carry):
        tops, ranks, scores = [], [], []
        for half in range(2):
            c0 = pl.multiple_of((hd * 2 + half) * N_KEYS, N_KEYS)
            s = _dot_nt(keys_ref[hd, half], q_ref[:, pl.ds(c0, N_KEYS)].astype(BF16))
            blocks = [s[SUBLANES * i:SUBLANES * (i + 1)] for i in range(n_blk)]
            taken, order = _extract_top(blocks, key_index, k16)
            tops.append(taken)
            ranks.append(order)
            scores.append(blocks)
        r1, r2 = tops
        r1_hi = stack_rows(r1[SUBLANES:])
        r2_blk = [stack_rows(r2[SUBLANES * i:SUBLANES * (i + 1)]) for i in range(half_blocks)]
        cblocks, cindex, cvalid = [], [], []
        for a in range(SUBLANES):
            nb = k16 // (a + 1)
            for bi in range(half_blocks):
                if bi * SUBLANES >= nb:
                    continue
                valid = row < float(nb - bi * SUBLANES)
                cblocks.append(jnp.where(valid, r1[a] + r2_blk[bi], NEG_BIG))
                cindex.append(row + float(a * k16 + bi * SUBLANES))
                cvalid.append((a, bi))
        cblocks.append(r1_hi + r2[0])
        cindex.append((row + float(SUBLANES)) * float(k16))
        _, corder = _extract_top(cblocks, cindex, k16)
        sel = [(o < float(k16)).astype(F32) for o in corder]
        cmax = r1[0] + r2[0]
        z = None
        for cb, sl in zip(cblocks, sel):
            part = jnp.sum(sl * jnp.exp(cb - cmax), axis=0, keepdims=True)
            z = part if z is None else z + part
        counts = []
        for a in range(SUBLANES):
            na = None
            for (ca, _), sl in zip(cvalid, sel[:-1]):
                if ca == a:
                    part = jnp.sum(sl, axis=0, keepdims=True)
                    na = part if na is None else na + part
            counts.append(na)
        tail = sel[-1]
        counts += [tail[i:i + 1] for i in range(SUBLANES)]
        inv_z = 1.0 / z
        for i in range(n_blk):
            rk1 = ranks[0][i]
            th = jnp.zeros_like(rk1)
            for a in range(k16):
                th = jnp.where(rk1 == float(a), counts[a], th)
            rows = slice(SUBLANES * i, SUBLANES * (i + 1))
            th_ref[hd, rows, :] = th
            e1_ref[hd, rows, :] = jnp.exp(jnp.minimum(scores[0][i] - r1[0], 0.0))
            rk_ref[hd, rows, :] = ranks[1][i]
            e2_ref[hd, rows, :] = jnp.exp(jnp.minimum(scores[1][i] - r2[0], 0.0)) * inv_z
        return carry

    lax.fori_loop(0, PEER_HEADS, head, 0)


def peer_route(qp, keys):
    t_all, qw = qp.shape
    tt = LANES
    side = jax.ShapeDtypeStruct((PEER_HEADS, N_KEYS, t_all), F32)
    side_spec = pl.BlockSpec((PEER_HEADS, N_KEYS, tt), lambda i: (0, 0, i))
    return pl.pallas_call(
        functools.partial(_peer_route_body, tt=tt),
        grid=(t_all // tt,),
        in_specs=[
            pl.BlockSpec((tt, qw), lambda i: (i, 0)),
            pl.BlockSpec(keys.shape, lambda i: (0, 0, 0, 0)),
        ],
        out_specs=[side_spec] * 4,
        out_shape=[side] * 4,
        compiler_params=_cparams(("arbitrary",)),
        name="peer_route",
    )(qp, keys)


def _peer_dense_body(h_ref, xn_ref, th_ref, e1_ref, rk_ref, e2_ref, u_ref, vt_ref, m5_ref,
                     o_ref, acc_scr, a_scr, *, ec):
    j = pl.program_id(1)

    @pl.when(j == 0)
    def _():
        acc_scr[...] = jnp.zeros(acc_scr.shape, F32)

    hid = _dot_nt(u_ref[...], xn_ref[...])
    act = jax.nn.gelu(hid, approximate=True)
    for ii in range(ec // N_KEYS):
        i = j * (ec // N_KEYS) + ii
        gate = None
        for hd in range(PEER_HEADS):
            th = th_ref[hd, pl.ds(i, 1), :]
            e1 = e1_ref[hd, pl.ds(i, 1), :]
            term = jnp.where(rk_ref[hd] < th, e2_ref[hd], 0.0) * e1
            gate = term if gate is None else gate + term
        rows = slice(ii * N_KEYS, (ii + 1) * N_KEYS)
        a_scr[rows, :] = (gate * act[rows]).astype(BF16)
    acc_scr[...] += _dot(vt_ref[...], a_scr[...])

    @pl.when(j == pl.num_programs(1) - 1)
    def _():
        o_ref[...] = h_ref[...] + m5_ref[0, 0] * acc_scr[...].T


def peer_dense(h2, seq_len, xn, th, e1, rk, e2, u, vt, mod):
    t_all, d = h2.shape
    n_exp = u.shape[0]
    tt = _tile(seq_len, (512, 256, 128))
    ec = 1024
    side_spec = pl.BlockSpec((PEER_HEADS, N_KEYS, tt), lambda i, j: (0, 0, i))
    return pl.pallas_call(
        functools.partial(_peer_dense_body, ec=ec),
        grid=(t_all // tt, n_exp // ec),
        in_specs=[
            pl.BlockSpec((tt, d), lambda i, j: (i, 0)),
            pl.BlockSpec((tt, d), lambda i, j: (i, 0)),
            side_spec, side_spec, side_spec, side_spec,
            pl.BlockSpec((ec, d), lambda i, j: (j, 0)),
            pl.BlockSpec((d, ec), lambda i, j: (0, j)),
            pl.BlockSpec((1, 1, 1, d), lambda i, j: ((i * tt) // seq_len, 5, 0, 0)),
        ],
        out_specs=pl.BlockSpec((tt, d), lambda i, j: (i, 0)),
        out_shape=jax.ShapeDtypeStruct((t_all, d), F32),
        scratch_shapes=[pltpu.VMEM((d, tt), F32), pltpu.VMEM((ec, tt), BF16)],
        compiler_params=_cparams(("arbitrary", "arbitrary")),
        name="peer_dense",
    )(h2, xn, th, e1, rk, e2, u, vt, mod)


def _rope_tables(n):
    n_rows = n // GRID_W
    rows = jnp.repeat(jnp.arange(n_rows, dtype=F32), GRID_W)
    cols = jnp.tile(jnp.arange(GRID_W, dtype=F32), n_rows)
    n_freq = ROPE_HALF // 2
    inv = ROPE_BASE ** (-jnp.arange(n_freq, dtype=F32) / n_freq)
    ang = jnp.concatenate([rows[:, None] * inv, cols[:, None] * inv], axis=-1)
    cos, sin = jnp.cos(ang), jnp.sin(ang)
    reps = LANES // HEAD_DIM
    cos_t = jnp.tile(jnp.concatenate([cos, cos], axis=-1), (1, reps))
    sin_t = jnp.tile(jnp.concatenate([-sin, sin], axis=-1), (1, reps))
    return cos_t, sin_t


def kernel(x, c, ctx, c_ctx, norm1_g, norm2_g, w_mod, b_mod, w_in, q_norm_g, k_norm_g, lam_vecs,
           attn_norm_g, pool_w, pool_scale, conv_w, conv_out_w, w_out, peer_wq, peer_keys, peer_u, peer_v):
    b, s, d = x.shape
    n_ctx = ctx.shape[1]
    depth = w_in.shape[0]
    qk_w = N_HEADS * 2 * HEAD_DIM
    assert b + 1 <= SUBLANES

    cc = jnp.zeros((SUBLANES, d), F32).at[:b].set(c).at[b].set(c_ctx)
    mods = modulation(cc, w_mod, b_mod).reshape(depth, SUBLANES, N_MOD, 1, d)

    cos_t, sin_t = _rope_tables(s)
    chunk = jnp.arange(qk_w) // HEAD_DIM
    block_diag = (chunk[:, None] == chunk[None, :]).astype(BF16)
    tile_gain = lambda g: jnp.tile(g, qk_w // HEAD_DIM).reshape(1, qk_w)

    w_in_b = w_in.astype(BF16)
    pool_w_b = pool_w.astype(BF16)
    conv_out_b = conv_out_w.astype(BF16)
    w_out_b = w_out.astype(BF16)
    peer_wq_b = peer_wq.astype(BF16)
    keys_b = peer_keys.astype(BF16)
    peer_u_b = peer_u.astype(BF16)
    peer_vt_b = jnp.swapaxes(peer_v, 1, 2).astype(BF16)

    def peer_block(hh, seq_len, mod, l):
        h2 = hh.reshape(-1, d)
        qp, xn = norm_matmul(h2, seq_len, norm2_g[l], mod, 3, 4, peer_wq_b[l], F32, emit_xn=True)
        th, e1, rk, e2 = peer_route(qp, keys_b[l])
        out = peer_dense(h2, seq_len, xn, th, e1, rk, e2, peer_u_b[l], peer_vt_b[l], mod)
        return out.reshape(hh.shape)

    def mixer_block(hh, attn, p3, mod, l):
        return mixer(hh, attn, p3, pool_w_b[l], pool_scale[l], conv_w[l], conv_out_b[l], w_out_b[l], mod)

    h, hc = x, ctx
    for l in range(depth):
        last = l == depth - 1
        lam_init = 0.8 - 0.6 * math.exp(-0.3 * l)
        mod = mods[l, :b]
        mod_c = jnp.broadcast_to(mods[l, b:b + 1], (b, N_MOD, 1, d))
        qg, kg = tile_gain(q_norm_g[l]), tile_gain(k_norm_g[l])

        p = norm_matmul(h.reshape(-1, d), s, norm1_g[l], mod, 0, 1, w_in_b[l], F32).reshape(b, s, -1)
        pc = norm_matmul(hc.reshape(-1, d), n_ctx, norm1_g[l], mod_c, 0, 1, w_in_b[l], F32).reshape(b, n_ctx, -1)
        q, k, vt = qkv_post(p, cos_t, sin_t, qg, kg, block_diag, rope=True)
        qc, kc, vtc = qkv_post(pc, cos_t, sin_t, qg, kg, block_diag, rope=False)
        k_all = jnp.concatenate([k, kc], axis=1)
        vt_all = jnp.concatenate([vt, vtc], axis=2)
        attn = attention(q, k_all, vt_all, lam_vecs[l], attn_norm_g[l], lam_init)
        h = mixer_block(h, attn, p, mod, l)
        if not last:
            attn_c = attention(qc, kc, vtc, lam_vecs[l], attn_norm_g[l], lam_init)
            hc = mixer_block(hc, attn_c, pc, mod_c, l)
        h = peer_block(h, s, mod, l)
        if not last:
            hc = peer_block(hc, n_ctx, mod_c, l)
    return h
```

```python
import functools
import math

import jax
import jax.numpy as jnp
from jax import lax
from jax.experimental import pallas as pl
from jax.experimental.pallas import tpu as pltpu

F32 = jnp.float32
BF16 = jnp.bfloat16

LANES = 128
SUBLANES = 8
BF16_SUBLANES = 16
VMEM_LIMIT_BYTES = 56 * 1024 * 1024

EPS = 1e-6
N_HEADS = 8
HEAD_DIM = 64
V_DIM = 2 * HEAD_DIM
ROPE_HALF = HEAD_DIM // 2
ROPE_BASE = 10000.0
GRID_W = 64
POOL_SIZES = (2, 4, 8, 16)
POOL_HALO = 8
N_MOD = 6
N_KEYS = 128
PEER_HEADS = 8
PEER_TOPK = 16
NEG_BIG = -1e30
LOG2_E = math.log2(math.e)
Q_SCALE = HEAD_DIM ** -0.5 * LOG2_E
MAX_FIXED_SCORE_BOUND = 60.0
SCORE_BOUND_SLACK = 1.01
ATTN_KV_UNROLL = 4


def _cparams(semantics):
    return pltpu.CompilerParams(dimension_semantics=semantics, vmem_limit_bytes=VMEM_LIMIT_BYTES)


def _tile(n, prefs):
    for t in prefs:
        if n % t == 0:
            return t
    raise ValueError(f"no tile among {prefs} divides {n}")


def _dot(a, b):
    return jnp.dot(a, b, preferred_element_type=F32)


def _dot_nt(a, b):
    return lax.dot_general(a, b, (((1,), (1,)), ((), ())), preferred_element_type=F32)


def _mod_body(c_ref, w_ref, b_ref, o_ref):
    c = c_ref[...]
    s = c * jax.nn.sigmoid(c)
    o_ref[0] = _dot(s.astype(BF16), w_ref[0].astype(BF16)) + b_ref[0]


def modulation(cc, w_mod, b_mod):
    depth, d, n = w_mod.shape
    tn = _tile(n, (1536, 1024, 512))
    return pl.pallas_call(
        _mod_body,
        grid=(depth, n // tn),
        in_specs=[
            pl.BlockSpec((SUBLANES, d), lambda l, j: (0, 0)),
            pl.BlockSpec((1, d, tn), lambda l, j: (l, 0, j)),
            pl.BlockSpec((1, 1, tn), lambda l, j: (l, 0, j)),
        ],
        out_specs=pl.BlockSpec((1, SUBLANES, tn), lambda l, j: (l, 0, j)),
        out_shape=jax.ShapeDtypeStruct((depth, SUBLANES, n), F32),
        compiler_params=_cparams(("arbitrary", "arbitrary")),
        name="modulation",
    )(cc, w_mod, b_mod.reshape(depth, 1, n))


def _norm_matmul_body(h_ref, g_ref, sh_ref, sc_ref, w_ref, o_ref, *rest, emit_xn):
    xn_scr = rest[-1]

    @pl.when(pl.program_id(1) == 0)
    def _():
        x = h_ref[...]
        ms = jnp.mean(x * x, axis=-1, keepdims=True)
        y = x * lax.rsqrt(ms + EPS) * g_ref[...]
        xn = y * (1.0 + sc_ref[0, 0]) + sh_ref[0, 0]
        xn_scr[...] = xn.astype(BF16)
        if emit_xn:
            rest[0][...] = xn_scr[...]

    o_ref[...] = _dot(xn_scr[...], w_ref[...]).astype(o_ref.dtype)


def norm_matmul(h2, seq_len, gain, mod, shift_idx, scale_idx, w, out_dtype, emit_xn=False):
    t_all, d = h2.shape
    n = w.shape[1]
    tt = _tile(seq_len, (512, 256, 128))
    tn = _tile(n, (2048, 1024))
    out_shape = [jax.ShapeDtypeStruct((t_all, n), out_dtype)]
    out_specs = [pl.BlockSpec((tt, tn), lambda i, j: (i, j))]
    if emit_xn:
        out_shape.append(jax.ShapeDtypeStruct((t_all, d), BF16))
        out_specs.append(pl.BlockSpec((tt, d), lambda i, j: (i, 0)))
    res = pl.pallas_call(
        functools.partial(_norm_matmul_body, emit_xn=emit_xn),
        grid=(t_all // tt, n // tn),
        in_specs=[
            pl.BlockSpec((tt, d), lambda i, j: (i, 0)),
            pl.BlockSpec((1, d), lambda i, j: (0, 0)),
            pl.BlockSpec((1, 1, 1, d), lambda i, j: ((i * tt) // seq_len, shift_idx, 0, 0)),
            pl.BlockSpec((1, 1, 1, d), lambda i, j: ((i * tt) // seq_len, scale_idx, 0, 0)),
            pl.BlockSpec((d, tn), lambda i, j: (0, j)),
        ],
        out_specs=out_specs,
        out_shape=out_shape,
        scratch_shapes=[pltpu.VMEM((tt, d), BF16)],
        compiler_params=_cparams(("arbitrary", "arbitrary")),
        name="norm_matmul",
    )(h2, gain.reshape(1, d), mod, mod, w)
    return res if emit_xn else res[0]


def _qkv_post_body(pq_ref, pk_ref, pv_ref, cos_ref, sin_ref, qg_ref, kg_ref, bd_ref,
                   q_ref, k_ref, vt_ref, *, rope):
    lane = lax.broadcasted_iota(jnp.int32, (pq_ref.shape[1], LANES), 1)
    first_half = (lane % HEAD_DIM) < ROPE_HALF

    def norm_rope(x, g, scale):
        ms = _dot((x * x).astype(BF16), bd_ref[...]) * (1.0 / HEAD_DIM)
        y = x * lax.rsqrt(ms + EPS) * g
        if scale != 1.0:
            y = y * scale
        if not rope:
            return y
        cos = cos_ref[...]
        sin = sin_ref[...]
        parts = []
        for hd in range(y.shape[1] // LANES):
            yh = y[:, hd * LANES:(hd + 1) * LANES]
            partner = jnp.where(first_half, pltpu.roll(yh, LANES - ROPE_HALF, 1), pltpu.roll(yh, ROPE_HALF, 1))
            parts.append(yh * cos + partner * sin)
        return jnp.concatenate(parts, axis=1)

    q_ref[0] = norm_rope(pq_ref[0].astype(F32), qg_ref[...], Q_SCALE).astype(BF16)
    k_ref[0] = norm_rope(pk_ref[0].astype(F32), kg_ref[...], 1.0).astype(BF16)
    vt_ref[0] = pv_ref[0].astype(F32).T.astype(BF16)


def qkv_post(p3, cos_t, sin_t, q_gain, k_gain, block_diag, rope):
    b, l, _ = p3.shape
    w = N_HEADS * V_DIM
    tt = _tile(l, (256, 128))
    col = lambda c: pl.BlockSpec((1, tt, w), lambda i, j, c=c: (i, j, c))
    return pl.pallas_call(
        functools.partial(_qkv_post_body, rope=rope),
        grid=(b, l // tt),
        in_specs=[
            col(0), col(1), col(2),
            pl.BlockSpec((tt, LANES), lambda i, j: (j, 0)),
            pl.BlockSpec((tt, LANES), lambda i, j: (j, 0)),
            pl.BlockSpec((1, w), lambda i, j: (0, 0)),
            pl.BlockSpec((1, w), lambda i, j: (0, 0)),
            pl.BlockSpec((w, w), lambda i, j: (0, 0)),
        ],
        out_specs=[
            pl.BlockSpec((1, tt, w), lambda i, j: (i, j, 0)),
            pl.BlockSpec((1, tt, w), lambda i, j: (i, j, 0)),
            pl.BlockSpec((1, w, tt), lambda i, j: (i, 0, j)),
        ],
        out_shape=[
            jax.ShapeDtypeStruct((b, l, w), BF16),
            jax.ShapeDtypeStruct((b, l, w), BF16),
            jax.ShapeDtypeStruct((b, w, l), BF16),
        ],
        compiler_params=_cparams(("arbitrary", "arbitrary")),
        name="qkv_post",
    )(p3, p3, p3, cos_t, sin_t, q_gain, k_gain, block_diag)


def _attention_body(mb_ref, lv_ref, g_ref, q_ref, k_ref, vt_ref, o_ref, m_scr, l_scr, acc_scr,
                    *, tq, tk, lk, lam_init, fixed_max, unroll):
    lane = lax.broadcasted_iota(jnp.int32, (tq, LANES), 1)
    q = q_ref[0].astype(F32)
    q2 = jnp.concatenate([jnp.where(lane < HEAD_DIM, q, 0.0), jnp.where(lane >= HEAD_DIM, q, 0.0)], axis=0)
    q2 = q2.astype(BF16)
    n = 2 * tq

    if not fixed_max:
        m_scr[...] = jnp.full(m_scr.shape, NEG_BIG, F32)
    l_scr[...] = jnp.zeros(l_scr.shape, F32)
    acc_scr[...] = jnp.zeros(acc_scr.shape, F32)

    def block(start, size):
        kb = k_ref[0, pl.ds(start, size), :]
        vb = vt_ref[0, :, pl.ds(start, size)]
        s = _dot_nt(kb, q2)
        if fixed_max:
            p = jnp.exp2(s - mb_ref[...])
            l_scr[...] += jnp.sum(p.reshape(size // SUBLANES, SUBLANES, n), axis=0)
            acc_scr[...] += _dot(vb, p.astype(BF16))
        else:
            m_old = m_scr[...]
            m_new = jnp.maximum(m_old, jnp.max(s, axis=0, keepdims=True))
            alpha = jnp.exp2(m_old - m_new)
            p = jnp.exp2(s - m_new)
            l_scr[...] = alpha * l_scr[...] + jnp.sum(p.reshape(size // SUBLANES, SUBLANES, n), axis=0)
            m_scr[...] = m_new
            acc_scr[...] = acc_scr[...] * alpha + _dot(vb, p.astype(BF16))

    n_full = lk // tk
    rem = lk - n_full * tk
    if n_full > 0:
        def step(j, carry):
            block(pl.multiple_of(j * tk, tk), tk)
            return carry
        lax.fori_loop(0, n_full, step, 0, unroll=math.gcd(unroll, n_full))
    if rem > 0:
        block(n_full * tk, rem)

    lv = lv_ref[...]
    lam = (jnp.exp(jnp.sum(lv[0:1] * lv[1:2], axis=1, keepdims=True))
           - jnp.exp(jnp.sum(lv[2:3] * lv[3:4], axis=1, keepdims=True)) + lam_init)
    acc = acc_scr[...]
    l = jnp.sum(l_scr[...], axis=0, keepdims=True)
    o = acc[:, :tq] / l[:, :tq] - lam * (acc[:, tq:] / l[:, tq:])
    ms = jnp.mean(o * o, axis=0, keepdims=True)
    y = o * lax.rsqrt(ms + EPS) * g_ref[...] * (1.0 - lam_init)
    o_ref[0] = y.T


def attention(q, k, vt, lam_vecs, out_gain, lam_init, score_bound):
    b, lq, w = q.shape
    lk = k.shape[1]
    tq = _tile(lq, (256, 128))
    tk = min(512, lk)

    def call(fixed_max):
        return _attention_call(q, k, vt, lam_vecs, out_gain, score_bound, b, lq, lk, w, tq, tk, lam_init, fixed_max)

    return lax.cond(score_bound <= MAX_FIXED_SCORE_BOUND, lambda: call(True), lambda: call(False))


def _attention_call(q, k, vt, lam_vecs, out_gain, score_bound, b, lq, lk, w, tq, tk, lam_init, fixed_max):
    return pl.pallas_call(
        functools.partial(_attention_body, tq=tq, tk=tk, lk=lk, lam_init=lam_init, fixed_max=fixed_max,
                          unroll=ATTN_KV_UNROLL),
        grid=(b, N_HEADS, lq // tq),
        in_specs=[
            pl.BlockSpec((1, 1), lambda i, h, j: (0, 0)),
            pl.BlockSpec((4, HEAD_DIM), lambda i, h, j: (0, 0)),
            pl.BlockSpec((V_DIM, 1), lambda i, h, j: (0, 0)),
            pl.BlockSpec((1, tq, V_DIM), lambda i, h, j: (i, j, h)),
            pl.BlockSpec((1, lk, V_DIM), lambda i, h, j: (i, 0, h)),
            pl.BlockSpec((1, V_DIM, lk), lambda i, h, j: (i, h, 0)),
        ],
        out_specs=pl.BlockSpec((1, tq, V_DIM), lambda i, h, j: (i, j, h)),
        out_shape=jax.ShapeDtypeStruct((b, lq, w), F32),
        scratch_shapes=[
            pltpu.VMEM((1, 2 * tq), F32),
            pltpu.VMEM((SUBLANES, 2 * tq), F32),
            pltpu.VMEM((V_DIM, 2 * tq), F32),
        ],
        compiler_params=_cparams(("arbitrary", "arbitrary", "arbitrary")),
        name="attention_fixed" if fixed_max else "attention_online",
    )(score_bound.reshape(1, 1).astype(F32), lam_vecs, out_gain.reshape(V_DIM, 1), q, k, vt)


def _mixer_body(h_ref, a_ref, u_ref, up_ref, un_ref, ch_ref, chp_ref, chn_ref, cb_ref,
                cc_ref, ccp_ref, ccn_ref, g0_ref, g1_ref, g2_ref,
                pw_ref, ps_ref, cw_ref, cow_ref, wo_ref, m2_ref, o_ref, *, tt, seq_len):
    t0 = pl.program_id(1) * tt
    rows = tt + 2 * POOL_HALO
    d = h_ref.shape[2]
    pos = lax.broadcasted_iota(jnp.int32, (rows, d), 0) + (t0 - POOL_HALO)
    in_seq = (pos >= 0) & (pos < seq_len)

    def extended(prev_ref, cur_ref, next_ref):
        e = jnp.concatenate([prev_ref[0].astype(F32), cur_ref[0].astype(F32), next_ref[0].astype(F32)], axis=0)
        return jnp.where(in_seq, e, 0.0)

    ue = extended(up_ref, u_ref, un_ref)
    u = ue[POOL_HALO:POOL_HALO + tt]
    gw = ue.shape[1] // len(POOL_SIZES)
    tpos = lax.broadcasted_iota(jnp.int32, (tt, gw), 0) + t0
    pooled = []
    for gi, wsz in enumerate(POOL_SIZES):
        lo = wsz // 2
        hi = wsz - 1 - lo
        e = ue[:, gi * gw:(gi + 1) * gw]
        acc = e
        span = 1
        while span < wsz:
            acc = acc + pltpu.roll(acc, rows - span, 0)
            span *= 2
        win = pltpu.roll(acc, lo, 0)[POOL_HALO:POOL_HALO + tt]
        cnt = (jnp.minimum(tpos + hi + 1, seq_len) - jnp.maximum(tpos - lo, 0)).astype(F32)
        pooled.append(win / cnt - u[:, gi * gw:(gi + 1) * gw])
    pool = jnp.concatenate(
        [_dot(pooled[gi].astype(BF16), pw_ref[gi]) for gi in range(len(POOL_SIZES))], axis=1) * ps_ref[...]

    che = extended(chp_ref, ch_ref, chn_ref) * extended(ccp_ref, cc_ref, ccn_ref)
    cw = cw_ref[...]
    conv = (cw[0:1] * pltpu.roll(che, 1, 0)[POOL_HALO:POOL_HALO + tt]
            + cw[1:2] * che[POOL_HALO:POOL_HALO + tt]
            + cw[2:3] * pltpu.roll(che, rows - 1, 0)[POOL_HALO:POOL_HALO + tt])
    conv = _dot((cb_ref[0].astype(F32) * conv).astype(BF16), cow_ref[...])

    merged = (jax.nn.sigmoid(g0_ref[0].astype(F32)) * a_ref[0]
              + jax.nn.sigmoid(g1_ref[0].astype(F32)) * pool
              + jax.nn.sigmoid(g2_ref[0].astype(F32)) * conv)
    mix = _dot(merged.astype(BF16), wo_ref[...])
    o_ref[0] = h_ref[0] + m2_ref[0, 0] * mix


def mixer(h3, attn, p3, pool_w, pool_scale, conv_w, conv_out_w, w_out, mod):
    b, l, d = h3.shape
    tt = _tile(l, (256, 128))
    hb = tt // POOL_HALO
    n_hb = l // POOL_HALO
    cur = lambda c: pl.BlockSpec((1, tt, d), lambda i, j, c=c: (i, j, c))
    prev = lambda c: pl.BlockSpec((1, POOL_HALO, d), lambda i, j, c=c: (i, jnp.maximum(j * hb - 1, 0), c))
    nxt = lambda c: pl.BlockSpec((1, POOL_HALO, d), lambda i, j, c=c: (i, jnp.minimum((j + 1) * hb, n_hb - 1), c))
    full = lambda shape: pl.BlockSpec(shape, lambda i, j: (0,) * len(shape))
    return pl.pallas_call(
        functools.partial(_mixer_body, tt=tt, seq_len=l),
        grid=(b, l // tt),
        in_specs=[
            cur(0), cur(0),
            cur(3), prev(3), nxt(3),
            cur(4), prev(4), nxt(4), cur(5),
            cur(6), prev(6), nxt(6),
            cur(7), cur(8), cur(9),
            full(pool_w.shape), full((1, d)), full(conv_w.shape), full((d, d)), full((d, d)),
            pl.BlockSpec((1, 1, 1, d), lambda i, j: (i, 2, 0, 0)),
        ],
        out_specs=pl.BlockSpec((1, tt, d), lambda i, j: (i, j, 0)),
        out_shape=jax.ShapeDtypeStruct((b, l, d), F32),
        compiler_params=_cparams(("arbitrary", "arbitrary")),
        name="mixer",
    )(h3, attn, p3, p3, p3, p3, p3, p3, p3, p3, p3, p3, p3, p3, p3,
      pool_w, pool_scale.reshape(1, d), conv_w, conv_out_w, w_out, mod)


def _sublane_all(op, x):
    for shift in (4, 2, 1):
        x = op(x, pltpu.roll(x, shift, 0))
    return x


def _tree(op, xs):
    xs = list(xs)
    while len(xs) > 1:
        xs = [op(xs[i], xs[i + 1]) if i + 1 < len(xs) else xs[i] for i in range(0, len(xs), 2)]
    return xs[0]


def _extract_top(blocks, index_blocks, n_take, tie_break, want_order=True):
    assert tie_break and want_order
    work = list(blocks)
    order = [jnp.full(b.shape, float(n_take), F32) for b in blocks]
    taken = []
    big = float(1 << 20)
    for a in range(n_take):
        m = work[0]
        for w in work[1:]:
            m = jnp.maximum(m, w)
        m = jnp.max(m, axis=0, keepdims=True)
        cand = [jnp.where(w == m, ib, big) for w, ib in zip(work, index_blocks)]
        first = cand[0]
        for c in cand[1:]:
            first = jnp.minimum(first, c)
        first = jnp.min(first, axis=0, keepdims=True)
        hit = [ib == first for ib in index_blocks]
        work = [jnp.where(ht, NEG_BIG * 2.0, w) for ht, w in zip(hit, work)]
        order = [jnp.where(ht, float(a), o) for ht, o in zip(hit, order)]
        taken.append(m)
    return taken, order, None


def _peer_route_body(q_ref, keys_ref, th_ref, e1_ref, rk_ref, e2_ref, *, tt):
    row = lax.broadcasted_iota(jnp.int32, (SUBLANES, tt), 0).astype(F32)
    n_blk = N_KEYS // SUBLANES
    key_index = [row + float(SUBLANES * i) for i in range(n_blk)]
    k16 = PEER_TOPK
    half_blocks = k16 // SUBLANES

    def stack_rows(vals):
        out = jnp.zeros((SUBLANES, tt), F32)
        for i, v in enumerate(vals):
            out = jnp.where(row == float(i), v, out)
        return out

    def route_head(hd, tie_break):
        tops, ranks, scores, bad = [], [], [], None
        for half in range(2):
            c0 = pl.multiple_of((hd * 2 + half) * N_KEYS, N_KEYS)
            s = _dot_nt(keys_ref[hd, half], q_ref[:, pl.ds(c0, N_KEYS)].astype(BF16))
            blocks = [s[SUBLANES * i:SUBLANES * (i + 1)] for i in range(n_blk)]
            taken, order, count = _extract_top(blocks, key_index, k16, tie_break,
                                               want_order=tie_break or half == 1)
            tops.append(taken)
            ranks.append(order)
            scores.append(blocks)
            if not tie_break:
                bad = count if bad is None else jnp.maximum(bad, count)
        r1, r2 = tops
        r1_hi = stack_rows(r1[SUBLANES:])
        r2_blk = [stack_rows(r2[SUBLANES * i:SUBLANES * (i + 1)]) for i in range(half_blocks)]
        cblocks, cindex, cvalid = [], [], []
        for a in range(SUBLANES):
            nb = k16 // (a + 1)
            for bi in range(half_blocks):
                if bi * SUBLANES >= nb:
                    continue
                valid = row < float(nb - bi * SUBLANES)
                cblocks.append(jnp.where(valid, r1[a] + r2_blk[bi], NEG_BIG))
                cindex.append(row + float(a * k16 + bi * SUBLANES))
                cvalid.append((a, bi))
        cblocks.append(r1_hi + r2[0])
        cindex.append((row + float(SUBLANES)) * float(k16))
        _, corder, ccount = _extract_top(cblocks, cindex, k16, tie_break)
        if not tie_break:
            bad = jnp.maximum(bad, ccount)
        sel = [(o < float(k16)).astype(F32) for o in corder]
        cmax = r1[0] + r2[0]
        z = None
        for cb, sl in zip(cblocks, sel):
            part = jnp.sum(sl * jnp.exp(cb - cmax), axis=0, keepdims=True)
            z = part if z is None else z + part
        counts = []
        for a in range(SUBLANES):
            na = None
            for (ca, _), sl in zip(cvalid, sel[:-1]):
                if ca == a:
                    part = jnp.sum(sl, axis=0, keepdims=True)
                    na = part if na is None else na + part
            counts.append(na)
        tail = sel[-1]
        counts += [tail[i:i + 1] for i in range(SUBLANES)]
        inv_z = 1.0 / z
        for i in range(n_blk):
            th = jnp.zeros((SUBLANES, tt), F32)
            for a in range(k16):
                is_a = (ranks[0][i] == float(a)) if tie_break else (scores[0][i] == r1[a])
                th = jnp.where(is_a, counts[a], th)
            rows = slice(SUBLANES * i, SUBLANES * (i + 1))
            th_ref[hd, rows, :] = th
            e1_ref[hd, rows, :] = jnp.exp(jnp.minimum(scores[0][i] - r1[0], 0.0))
        pack = BF16_SUBLANES // SUBLANES
        for i in range(n_blk // pack):
            rows = slice(BF16_SUBLANES * i, BF16_SUBLANES * (i + 1))
            rk_ref[hd, rows, :] = jnp.concatenate(ranks[1][pack * i:pack * (i + 1)], axis=0).astype(rk_ref.dtype)
            e2 = [jnp.exp(jnp.minimum(sc - r2[0], 0.0)) * inv_z for sc in scores[1][pack * i:pack * (i + 1)]]
            e2_ref[hd, rows, :] = jnp.concatenate(e2, axis=0).astype(e2_ref.dtype)
        return bad

    def head(hd, carry):
        route_head(hd, tie_break=True)
        return carry

    lax.fori_loop(0, PEER_HEADS, head, 0)


def peer_route(qp, keys):
    t_all, qw = qp.shape
    tt = _tile(t_all, (512, 256, 128))
    side = lambda dt: jax.ShapeDtypeStruct((PEER_HEADS, N_KEYS, t_all), dt)
    side_spec = pl.BlockSpec((PEER_HEADS, N_KEYS, tt), lambda i: (0, 0, i))
    return pl.pallas_call(
        functools.partial(_peer_route_body, tt=tt),
        grid=(t_all // tt,),
        in_specs=[
            pl.BlockSpec((tt, qw), lambda i: (i, 0)),
            pl.BlockSpec(keys.shape, lambda i: (0, 0, 0, 0)),
        ],
        out_specs=[side_spec] * 4,
        out_shape=[side(F32)] * 4,
        compiler_params=_cparams(("arbitrary",)),
        name="peer_route",
    )(qp, keys)


def _peer_dense_body(h_ref, xn_ref, th_ref, e1_ref, rk_ref, e2_ref, u_ref, vt_ref, m5_ref,
                     o_ref, acc_scr, a_scr, *, ec):
    j = pl.program_id(1)

    @pl.when(j == 0)
    def _():
        acc_scr[...] = jnp.zeros(acc_scr.shape, F32)

    hid = _dot_nt(u_ref[...], xn_ref[...])
    act = jax.nn.gelu(hid, approximate=True)
    for ii in range(ec // N_KEYS):
        i = j * (ec // N_KEYS) + ii
        gate = None
        for hd in range(PEER_HEADS):
            th = th_ref[hd, pl.ds(i, 1), :]
            e1 = e1_ref[hd, pl.ds(i, 1), :]
            term = jnp.where(rk_ref[hd] < th, e2_ref[hd], 0.0) * e1
            gate = term if gate is None else gate + term
        rows = slice(ii * N_KEYS, (ii + 1) * N_KEYS)
        a_scr[rows, :] = (gate * act[rows]).astype(BF16)
    acc_scr[...] += _dot(vt_ref[...], a_scr[...])

    @pl.when(j == pl.num_programs(1) - 1)
    def _():
        o_ref[...] = h_ref[...] + m5_ref[0, 0] * acc_scr[...].T


def peer_dense(h2, seq_len, xn, th, e1, rk, e2, u, vt, mod):
    t_all, d = h2.shape
    n_exp = u.shape[0]
    tt = _tile(seq_len, (512, 256, 128))
    ec = 1024
    side_spec = pl.BlockSpec((PEER_HEADS, N_KEYS, tt), lambda i, j: (0, 0, i))
    return pl.pallas_call(
        functools.partial(_peer_dense_body, ec=ec),
        grid=(t_all // tt, n_exp // ec),
        in_specs=[
            pl.BlockSpec((tt, d), lambda i, j: (i, 0)),
            pl.BlockSpec((tt, d), lambda i, j: (i, 0)),
            side_spec, side_spec, side_spec, side_spec,
            pl.BlockSpec((ec, d), lambda i, j: (j, 0)),
            pl.BlockSpec((d, ec), lambda i, j: (0, j)),
            pl.BlockSpec((1, 1, 1, d), lambda i, j: ((i * tt) // seq_len, 5, 0, 0)),
        ],
        out_specs=pl.BlockSpec((tt, d), lambda i, j: (i, 0)),
        out_shape=jax.ShapeDtypeStruct((t_all, d), F32),
        scratch_shapes=[pltpu.VMEM((d, tt), F32), pltpu.VMEM((ec, tt), BF16)],
        compiler_params=_cparams(("arbitrary", "arbitrary")),
        name="peer_dense",
    )(h2, xn, th, e1, rk, e2, u, vt, mod)


def _rope_tables(n):
    n_rows = n // GRID_W
    rows = jnp.repeat(jnp.arange(n_rows, dtype=F32), GRID_W)
    cols = jnp.tile(jnp.arange(GRID_W, dtype=F32), n_rows)
    n_freq = ROPE_HALF // 2
    inv = ROPE_BASE ** (-jnp.arange(n_freq, dtype=F32) / n_freq)
    ang = jnp.concatenate([rows[:, None] * inv, cols[:, None] * inv], axis=-1)
    cos, sin = jnp.cos(ang), jnp.sin(ang)
    reps = LANES // HEAD_DIM
    cos_t = jnp.tile(jnp.concatenate([cos, cos], axis=-1), (1, reps))
    sin_t = jnp.tile(jnp.concatenate([-sin, sin], axis=-1), (1, reps))
    return cos_t, sin_t


def kernel(x, c, ctx, c_ctx, norm1_g, norm2_g, w_mod, b_mod, w_in, q_norm_g, k_norm_g, lam_vecs,
           attn_norm_g, pool_w, pool_scale, conv_w, conv_out_w, w_out, peer_wq, peer_keys, peer_u, peer_v):
    b, s, d = x.shape
    n_ctx = ctx.shape[1]
    depth = w_in.shape[0]
    qk_w = N_HEADS * 2 * HEAD_DIM
    assert b + 1 <= SUBLANES

    cc = jnp.zeros((SUBLANES, d), F32).at[:b].set(c).at[b].set(c_ctx)
    mods = modulation(cc, w_mod, b_mod).reshape(depth, SUBLANES, N_MOD, 1, d)

    cos_t, sin_t = _rope_tables(s)
    chunk = jnp.arange(qk_w) // HEAD_DIM
    block_diag = (chunk[:, None] == chunk[None, :]).astype(BF16)
    tile_gain = lambda g: jnp.tile(g, qk_w // HEAD_DIM).reshape(1, qk_w)

    w_in_b = w_in.astype(BF16)
    pool_w_b = pool_w.astype(BF16)
    conv_out_b = conv_out_w.astype(BF16)
    w_out_b = w_out.astype(BF16)
    peer_wq_b = peer_wq.astype(BF16)
    keys_b = peer_keys.astype(BF16)
    peer_u_b = peer_u.astype(BF16)
    peer_vt_b = jnp.swapaxes(peer_v, 1, 2).astype(BF16)

    def peer_block(hh, seq_len, mod, l):
        h2 = hh.reshape(-1, d)
        qp, xn = norm_matmul(h2, seq_len, norm2_g[l], mod, 3, 4, peer_wq_b[l], F32, emit_xn=True)
        th, e1, rk, e2 = peer_route(qp, keys_b[l])
        out = peer_dense(h2, seq_len, xn, th, e1, rk, e2, peer_u_b[l], peer_vt_b[l], mod)
        return out.reshape(hh.shape)

    def mixer_block(hh, attn, p3, mod, l):
        return mixer(hh, attn, p3, pool_w_b[l], pool_scale[l], conv_w[l], conv_out_b[l], w_out_b[l], mod)

    h, hc = x, ctx
    for l in range(depth):
        last = l == depth - 1
        lam_init = 0.8 - 0.6 * math.exp(-0.3 * l)
        mod = mods[l, :b]
        mod_c = jnp.broadcast_to(mods[l, b:b + 1], (b, N_MOD, 1, d))
        qg, kg = tile_gain(q_norm_g[l]), tile_gain(k_norm_g[l])

        p = norm_matmul(h.reshape(-1, d), s, norm1_g[l], mod, 0, 1, w_in_b[l], F32).reshape(b, s, -1)
        pc = norm_matmul(hc.reshape(-1, d), n_ctx, norm1_g[l], mod_c, 0, 1, w_in_b[l], F32).reshape(b, n_ctx, -1)
        q, k, vt = qkv_post(p, cos_t, sin_t, qg, kg, block_diag, rope=True)
        qc, kc, vtc = qkv_post(pc, cos_t, sin_t, qg, kg, block_diag, rope=False)
        k_all = jnp.concatenate([k, kc], axis=1)
        vt_all = jnp.concatenate([vt, vtc], axis=2)
        score_bound = (HEAD_DIM * Q_SCALE * SCORE_BOUND_SLACK
                       * jnp.max(jnp.abs(q_norm_g[l])) * jnp.max(jnp.abs(k_norm_g[l])))
        attn = attention(q, k_all, vt_all, lam_vecs[l], attn_norm_g[l], lam_init, score_bound)
        h = mixer_block(h, attn, p, mod, l)
        if not last:
            attn_c = attention(qc, kc, vtc, lam_vecs[l], attn_norm_g[l], lam_init, score_bound)
            hc = mixer_block(hc, attn_c, pc, mod_c, l)
        h = peer_block(h, s, mod, l)
        if not last:
            hc = peer_block(hc, n_ctx, mod_c, l)
    return h
```

```python
import functools
import math

import jax
import jax.numpy as jnp
from jax import lax
from jax.experimental import pallas as pl
from jax.experimental.pallas import tpu as pltpu

F32 = jnp.float32
BF16 = jnp.bfloat16

LANES = 128
SUBLANES = 8
BF16_SUBLANES = 16
VMEM_LIMIT_BYTES = 56 * 1024 * 1024

EPS = 1e-6
N_HEADS = 8
HEAD_DIM = 64
V_DIM = 2 * HEAD_DIM
ROPE_HALF = HEAD_DIM // 2
ROPE_BASE = 10000.0
GRID_W = 64
POOL_SIZES = (2, 4, 8, 16)
POOL_HALO = 8
N_MOD = 6
N_KEYS = 128
PEER_HEADS = 8
PEER_TOPK = 16
NEG_BIG = -1e30
LOG2_E = math.log2(math.e)
Q_SCALE = HEAD_DIM ** -0.5 * LOG2_E
MAX_FIXED_SCORE_BOUND = 60.0
SCORE_BOUND_SLACK = 1.01
ATTN_KV_UNROLL = 8


def _cparams(semantics):
    return pltpu.CompilerParams(dimension_semantics=semantics, vmem_limit_bytes=VMEM_LIMIT_BYTES)


def _tile(n, prefs):
    for t in prefs:
        if n % t == 0:
            return t
    raise ValueError(f"no tile among {prefs} divides {n}")


def _dot(a, b):
    return jnp.dot(a, b, preferred_element_type=F32)


def _dot_nt(a, b):
    return lax.dot_general(a, b, (((1,), (1,)), ((), ())), preferred_element_type=F32)


def _mod_body(c_ref, w_ref, b_ref, o_ref):
    c = c_ref[...]
    s = c * jax.nn.sigmoid(c)
    o_ref[0] = _dot(s.astype(BF16), w_ref[0].astype(BF16)) + b_ref[0]


def modulation(cc, w_mod, b_mod):
    depth, d, n = w_mod.shape
    tn = _tile(n, (1536, 1024, 512))
    return pl.pallas_call(
        _mod_body,
        grid=(depth, n // tn),
        in_specs=[
            pl.BlockSpec((SUBLANES, d), lambda l, j: (0, 0)),
            pl.BlockSpec((1, d, tn), lambda l, j: (l, 0, j)),
            pl.BlockSpec((1, 1, tn), lambda l, j: (l, 0, j)),
        ],
        out_specs=pl.BlockSpec((1, SUBLANES, tn), lambda l, j: (l, 0, j)),
        out_shape=jax.ShapeDtypeStruct((depth, SUBLANES, n), F32),
        compiler_params=_cparams(("arbitrary", "arbitrary")),
        name="modulation",
    )(cc, w_mod, b_mod.reshape(depth, 1, n))


def _norm_matmul_body(h_ref, g_ref, sh_ref, sc_ref, w_ref, o_ref, *rest, emit_xn):
    xn_scr = rest[-1]

    @pl.when(pl.program_id(1) == 0)
    def _():
        x = h_ref[...]
        ms = jnp.mean(x * x, axis=-1, keepdims=True)
        y = x * lax.rsqrt(ms + EPS) * g_ref[...]
        xn = y * (1.0 + sc_ref[0, 0]) + sh_ref[0, 0]
        xn_scr[...] = xn.astype(BF16)
        if emit_xn:
            rest[0][...] = xn_scr[...]

    o_ref[...] = _dot(xn_scr[...], w_ref[...]).astype(o_ref.dtype)


def norm_matmul(h2, seq_len, gain, mod, shift_idx, scale_idx, w, out_dtype, emit_xn=False):
    t_all, d = h2.shape
    n = w.shape[1]
    tt = _tile(seq_len, (512, 256, 128))
    tn = _tile(n, (2048, 1024))
    out_shape = [jax.ShapeDtypeStruct((t_all, n), out_dtype)]
    out_specs = [pl.BlockSpec((tt, tn), lambda i, j: (i, j))]
    if emit_xn:
        out_shape.append(jax.ShapeDtypeStruct((t_all, d), BF16))
        out_specs.append(pl.BlockSpec((tt, d), lambda i, j: (i, 0)))
    res = pl.pallas_call(
        functools.partial(_norm_matmul_body, emit_xn=emit_xn),
        grid=(t_all // tt, n // tn),
        in_specs=[
            pl.BlockSpec((tt, d), lambda i, j: (i, 0)),
            pl.BlockSpec((1, d), lambda i, j: (0, 0)),
            pl.BlockSpec((1, 1, 1, d), lambda i, j: ((i * tt) // seq_len, shift_idx, 0, 0)),
            pl.BlockSpec((1, 1, 1, d), lambda i, j: ((i * tt) // seq_len, scale_idx, 0, 0)),
            pl.BlockSpec((d, tn), lambda i, j: (0, j)),
        ],
        out_specs=out_specs,
        out_shape=out_shape,
        scratch_shapes=[pltpu.VMEM((tt, d), BF16)],
        compiler_params=_cparams(("arbitrary", "arbitrary")),
        name="norm_matmul",
    )(h2, gain.reshape(1, d), mod, mod, w)
    return res if emit_xn else res[0]


def _qkv_post_body(pq_ref, pk_ref, pv_ref, cos_ref, sin_ref, qg_ref, kg_ref, bd_ref,
                   q_ref, k_ref, vt_ref, *, rope):
    lane = lax.broadcasted_iota(jnp.int32, (pq_ref.shape[1], LANES), 1)
    first_half = (lane % HEAD_DIM) < ROPE_HALF

    def norm_rope(x, g, scale):
        ms = _dot((x * x).astype(BF16), bd_ref[...]) * (1.0 / HEAD_DIM)
        y = x * lax.rsqrt(ms + EPS) * g
        if scale != 1.0:
            y = y * scale
        if not rope:
            return y
        cos = cos_ref[...]
        sin = sin_ref[...]
        parts = []
        for hd in range(y.shape[1] // LANES):
            yh = y[:, hd * LANES:(hd + 1) * LANES]
            partner = jnp.where(first_half, pltpu.roll(yh, LANES - ROPE_HALF, 1), pltpu.roll(yh, ROPE_HALF, 1))
            parts.append(yh * cos + partner * sin)
        return jnp.concatenate(parts, axis=1)

    q_ref[0] = norm_rope(pq_ref[0].astype(F32), qg_ref[...], Q_SCALE).astype(BF16)
    k_ref[0] = norm_rope(pk_ref[0].astype(F32), kg_ref[...], 1.0).astype(BF16)
    vt_ref[0] = pv_ref[0].astype(F32).T.astype(BF16)


def qkv_post(p3, cos_t, sin_t, q_gain, k_gain, block_diag, rope):
    b, l, _ = p3.shape
    w = N_HEADS * V_DIM
    tt = _tile(l, (256, 128))
    col = lambda c: pl.BlockSpec((1, tt, w), lambda i, j, c=c: (i, j, c))
    return pl.pallas_call(
        functools.partial(_qkv_post_body, rope=rope),
        grid=(b, l // tt),
        in_specs=[
            col(0), col(1), col(2),
            pl.BlockSpec((tt, LANES), lambda i, j: (j, 0)),
            pl.BlockSpec((tt, LANES), lambda i, j: (j, 0)),
            pl.BlockSpec((1, w), lambda i, j: (0, 0)),
            pl.BlockSpec((1, w), lambda i, j: (0, 0)),
            pl.BlockSpec((w, w), lambda i, j: (0, 0)),
        ],
        out_specs=[
            pl.BlockSpec((1, tt, w), lambda i, j: (i, j, 0)),
            pl.BlockSpec((1, tt, w), lambda i, j: (i, j, 0)),
            pl.BlockSpec((1, w, tt), lambda i, j: (i, 0, j)),
        ],
        out_shape=[
            jax.ShapeDtypeStruct((b, l, w), BF16),
            jax.ShapeDtypeStruct((b, l, w), BF16),
            jax.ShapeDtypeStruct((b, w, l), BF16),
        ],
        compiler_params=_cparams(("arbitrary", "arbitrary")),
        name="qkv_post",
    )(p3, p3, p3, cos_t, sin_t, q_gain, k_gain, block_diag)


def _attention_body(mb_ref, lv_ref, g_ref, q_ref, k_ref, vt_ref, o_ref, m_scr, l_scr, acc_scr,
                    *, tq, tk, lk, lam_init, fixed_max, unroll):
    lane = lax.broadcasted_iota(jnp.int32, (tq, LANES), 1)
    q = q_ref[0].astype(F32)
    q2 = jnp.concatenate([jnp.where(lane < HEAD_DIM, q, 0.0), jnp.where(lane >= HEAD_DIM, q, 0.0)], axis=0)
    q2 = q2.astype(BF16)
    n = 2 * tq

    if not fixed_max:
        m_scr[...] = jnp.full(m_scr.shape, NEG_BIG, F32)
    l_scr[...] = jnp.zeros(l_scr.shape, F32)
    acc_scr[...] = jnp.zeros(acc_scr.shape, F32)

    def block(start, size):
        kb = k_ref[0, pl.ds(start, size), :]
        vb = vt_ref[0, :, pl.ds(start, size)]
        s = _dot_nt(kb, q2)
        if fixed_max:
            p = jnp.exp2(s - mb_ref[...])
            l_scr[...] += jnp.sum(p.reshape(size // SUBLANES, SUBLANES, n), axis=0)
            acc_scr[...] += _dot(vb, p.astype(BF16))
        else:
            m_old = m_scr[...]
            m_new = jnp.maximum(m_old, jnp.max(s, axis=0, keepdims=True))
            alpha = jnp.exp2(m_old - m_new)
            p = jnp.exp2(s - m_new)
            l_scr[...] = alpha * l_scr[...] + jnp.sum(p.reshape(size // SUBLANES, SUBLANES, n), axis=0)
            m_scr[...] = m_new
            acc_scr[...] = acc_scr[...] * alpha + _dot(vb, p.astype(BF16))

    n_full = lk // tk
    rem = lk - n_full * tk
    if n_full > 0:
        def step(j, carry):
            block(pl.multiple_of(j * tk, tk), tk)
            return carry
        lax.fori_loop(0, n_full, step, 0, unroll=math.gcd(unroll, n_full))
    if rem > 0:
        block(n_full * tk, rem)

    lv = lv_ref[...]
    lam = (jnp.exp(jnp.sum(lv[0:1] * lv[1:2], axis=1, keepdims=True))
           - jnp.exp(jnp.sum(lv[2:3] * lv[3:4], axis=1, keepdims=True)) + lam_init)
    acc = acc_scr[...]
    l = jnp.sum(l_scr[...], axis=0, keepdims=True)
    o = acc[:, :tq] / l[:, :tq] - lam * (acc[:, tq:] / l[:, tq:])
    ms = jnp.mean(o * o, axis=0, keepdims=True)
    y = o * lax.rsqrt(ms + EPS) * g_ref[...] * (1.0 - lam_init)
    o_ref[0] = y.T


def attention(q, k, vt, lam_vecs, out_gain, lam_init, score_bound):
    b, lq, w = q.shape
    lk = k.shape[1]
    tq = _tile(lq, (256, 128))
    tk = min(512, lk)

    def call(fixed_max):
        return _attention_call(q, k, vt, lam_vecs, out_gain, score_bound, b, lq, lk, w, tq, tk, lam_init, fixed_max)

    return lax.cond(score_bound <= MAX_FIXED_SCORE_BOUND, lambda: call(True), lambda: call(False))


def _attention_call(q, k, vt, lam_vecs, out_gain, score_bound, b, lq, lk, w, tq, tk, lam_init, fixed_max):
    return pl.pallas_call(
        functools.partial(_attention_body, tq=tq, tk=tk, lk=lk, lam_init=lam_init, fixed_max=fixed_max,
                          unroll=ATTN_KV_UNROLL),
        grid=(b, N_HEADS, lq // tq),
        in_specs=[
            pl.BlockSpec((1, 1), lambda i, h, j: (0, 0)),
            pl.BlockSpec((4, HEAD_DIM), lambda i, h, j: (0, 0)),
            pl.BlockSpec((V_DIM, 1), lambda i, h, j: (0, 0)),
            pl.BlockSpec((1, tq, V_DIM), lambda i, h, j: (i, j, h)),
            pl.BlockSpec((1, lk, V_DIM), lambda i, h, j: (i, 0, h)),
            pl.BlockSpec((1, V_DIM, lk), lambda i, h, j: (i, h, 0)),
        ],
        out_specs=pl.BlockSpec((1, tq, V_DIM), lambda i, h, j: (i, j, h)),
        out_shape=jax.ShapeDtypeStruct((b, lq, w), F32),
        scratch_shapes=[
            pltpu.VMEM((1, 2 * tq), F32),
            pltpu.VMEM((SUBLANES, 2 * tq), F32),
            pltpu.VMEM((V_DIM, 2 * tq), F32),
        ],
        compiler_params=_cparams(("arbitrary", "arbitrary", "arbitrary")),
        name="attention_fixed" if fixed_max else "attention_online",
    )(score_bound.reshape(1, 1).astype(F32), lam_vecs, out_gain.reshape(V_DIM, 1), q, k, vt)


def _mixer_body(h_ref, a_ref, u_ref, up_ref, un_ref, ch_ref, chp_ref, chn_ref, cb_ref,
                cc_ref, ccp_ref, ccn_ref, g0_ref, g1_ref, g2_ref,
                pw_ref, ps_ref, cw_ref, cow_ref, wo_ref, m2_ref, o_ref, *, tt, seq_len):
    t0 = pl.program_id(1) * tt
    rows = tt + 2 * POOL_HALO
    d = h_ref.shape[2]
    pos = lax.broadcasted_iota(jnp.int32, (rows, d), 0) + (t0 - POOL_HALO)
    in_seq = (pos >= 0) & (pos < seq_len)

    def extended(prev_ref, cur_ref, next_ref):
        e = jnp.concatenate([prev_ref[0].astype(F32), cur_ref[0].astype(F32), next_ref[0].astype(F32)], axis=0)
        return jnp.where(in_seq, e, 0.0)

    ue = extended(up_ref, u_ref, un_ref)
    u = ue[POOL_HALO:POOL_HALO + tt]
    gw = ue.shape[1] // len(POOL_SIZES)
    tpos = lax.broadcasted_iota(jnp.int32, (tt, gw), 0) + t0
    pooled = []
    for gi, wsz in enumerate(POOL_SIZES):
        lo = wsz // 2
        hi = wsz - 1 - lo
        e = ue[:, gi * gw:(gi + 1) * gw]
        acc = e
        span = 1
        while span < wsz:
            acc = acc + pltpu.roll(acc, rows - span, 0)
            span *= 2
        win = pltpu.roll(acc, lo, 0)[POOL_HALO:POOL_HALO + tt]
        cnt = (jnp.minimum(tpos + hi + 1, seq_len) - jnp.maximum(tpos - lo, 0)).astype(F32)
        pooled.append(win / cnt - u[:, gi * gw:(gi + 1) * gw])
    pool = jnp.concatenate(
        [_dot(pooled[gi].astype(BF16), pw_ref[gi]) for gi in range(len(POOL_SIZES))], axis=1) * ps_ref[...]

    che = extended(chp_ref, ch_ref, chn_ref) * extended(ccp_ref, cc_ref, ccn_ref)
    cw = cw_ref[...]
    conv = (cw[0:1] * pltpu.roll(che, 1, 0)[POOL_HALO:POOL_HALO + tt]
            + cw[1:2] * che[POOL_HALO:POOL_HALO + tt]
            + cw[2:3] * pltpu.roll(che, rows - 1, 0)[POOL_HALO:POOL_HALO + tt])
    conv = _dot((cb_ref[0].astype(F32) * conv).astype(BF16), cow_ref[...])

    merged = (jax.nn.sigmoid(g0_ref[0].astype(F32)) * a_ref[0]
              + jax.nn.sigmoid(g1_ref[0].astype(F32)) * pool
              + jax.nn.sigmoid(g2_ref[0].astype(F32)) * conv)
    mix = _dot(merged.astype(BF16), wo_ref[...])
    o_ref[0] = h_ref[0] + m2_ref[0, 0] * mix


def mixer(h3, attn, p3, pool_w, pool_scale, conv_w, conv_out_w, w_out, mod):
    b, l, d = h3.shape
    tt = _tile(l, (256, 128))
    hb = tt // POOL_HALO
    n_hb = l // POOL_HALO
    cur = lambda c: pl.BlockSpec((1, tt, d), lambda i, j, c=c: (i, j, c))
    prev = lambda c: pl.BlockSpec((1, POOL_HALO, d), lambda i, j, c=c: (i, jnp.maximum(j * hb - 1, 0), c))
    nxt = lambda c: pl.BlockSpec((1, POOL_HALO, d), lambda i, j, c=c: (i, jnp.minimum((j + 1) * hb, n_hb - 1), c))
    full = lambda shape: pl.BlockSpec(shape, lambda i, j: (0,) * len(shape))
    return pl.pallas_call(
        functools.partial(_mixer_body, tt=tt, seq_len=l),
        grid=(b, l // tt),
        in_specs=[
            cur(0), cur(0),
            cur(3), prev(3), nxt(3),
            cur(4), prev(4), nxt(4), cur(5),
            cur(6), prev(6), nxt(6),
            cur(7), cur(8), cur(9),
            full(pool_w.shape), full((1, d)), full(conv_w.shape), full((d, d)), full((d, d)),
            pl.BlockSpec((1, 1, 1, d), lambda i, j: (i, 2, 0, 0)),
        ],
        out_specs=pl.BlockSpec((1, tt, d), lambda i, j: (i, j, 0)),
        out_shape=jax.ShapeDtypeStruct((b, l, d), F32),
        compiler_params=_cparams(("arbitrary", "arbitrary")),
        name="mixer",
    )(h3, attn, p3, p3, p3, p3, p3, p3, p3, p3, p3, p3, p3, p3, p3,
      pool_w, pool_scale.reshape(1, d), conv_w, conv_out_w, w_out, mod)


def _sublane_all(op, x):
    for shift in (4, 2, 1):
        x = op(x, pltpu.roll(x, shift, 0))
    return x


def _tree(op, xs):
    xs = list(xs)
    while len(xs) > 1:
        xs = [op(xs[i], xs[i + 1]) if i + 1 < len(xs) else xs[i] for i in range(0, len(xs), 2)]
    return xs[0]


def _extract_top(blocks, index_blocks, n_take, tie_break, want_order=True):
    assert tie_break and want_order
    work = list(blocks)
    order = [jnp.full(b.shape, float(n_take), F32) for b in blocks]
    taken = []
    big = float(1 << 20)
    for a in range(n_take):
        m = work[0]
        for w in work[1:]:
            m = jnp.maximum(m, w)
        m = jnp.max(m, axis=0, keepdims=True)
        cand = [jnp.where(w == m, ib, big) for w, ib in zip(work, index_blocks)]
        first = cand[0]
        for c in cand[1:]:
            first = jnp.minimum(first, c)
        first = jnp.min(first, axis=0, keepdims=True)
        hit = [ib == first for ib in index_blocks]
        work = [jnp.where(ht, NEG_BIG * 2.0, w) for ht, w in zip(hit, work)]
        order = [jnp.where(ht, float(a), o) for ht, o in zip(hit, order)]
        taken.append(m)
    return taken, order, None


def _peer_route_body(q_ref, keys_ref, th_ref, e1_ref, rk_ref, e2_ref, *, tt):
    row = lax.broadcasted_iota(jnp.int32, (SUBLANES, tt), 0).astype(F32)
    n_blk = N_KEYS // SUBLANES
    key_index = [row + float(SUBLANES * i) for i in range(n_blk)]
    k16 = PEER_TOPK
    half_blocks = k16 // SUBLANES

    def stack_rows(vals):
        out = jnp.zeros((SUBLANES, tt), F32)
        for i, v in enumerate(vals):
            out = jnp.where(row == float(i), v, out)
        return out

    def route_head(hd, tie_break):
        tops, ranks, scores, bad = [], [], [], None
        for half in range(2):
            c0 = pl.multiple_of((hd * 2 + half) * N_KEYS, N_KEYS)
            s = _dot_nt(keys_ref[hd, half], q_ref[:, pl.ds(c0, N_KEYS)].astype(BF16))
            blocks = [s[SUBLANES * i:SUBLANES * (i + 1)] for i in range(n_blk)]
            taken, order, count = _extract_top(blocks, key_index, k16, tie_break,
                                               want_order=tie_break or half == 1)
            tops.append(taken)
            ranks.append(order)
            scores.append(blocks)
            if not tie_break:
                bad = count if bad is None else jnp.maximum(bad, count)
        r1, r2 = tops
        r1_hi = stack_rows(r1[SUBLANES:])
        r2_blk = [stack_rows(r2[SUBLANES * i:SUBLANES * (i + 1)]) for i in range(half_blocks)]
        cblocks, cindex, cvalid = [], [], []
        for a in range(SUBLANES):
            nb = k16 // (a + 1)
            for bi in range(half_blocks):
                if bi * SUBLANES >= nb:
                    continue
                valid = row < float(nb - bi * SUBLANES)
                cblocks.append(jnp.where(valid, r1[a] + r2_blk[bi], NEG_BIG))
                cindex.append(row + float(a * k16 + bi * SUBLANES))
                cvalid.append((a, bi))
        cblocks.append(r1_hi + r2[0])
        cindex.append((row + float(SUBLANES)) * float(k16))
        _, corder, ccount = _extract_top(cblocks, cindex, k16, tie_break)
        if not tie_break:
            bad = jnp.maximum(bad, ccount)
        sel = [(o < float(k16)).astype(F32) for o in corder]
        cmax = r1[0] + r2[0]
        z = None
        for cb, sl in zip(cblocks, sel):
            part = jnp.sum(sl * jnp.exp(cb - cmax), axis=0, keepdims=True)
            z = part if z is None else z + part
        counts = []
        for a in range(SUBLANES):
            na = None
            for (ca, _), sl in zip(cvalid, sel[:-1]):
                if ca == a:
                    part = jnp.sum(sl, axis=0, keepdims=True)
                    na = part if na is None else na + part
            counts.append(na)
        tail = sel[-1]
        counts += [tail[i:i + 1] for i in range(SUBLANES)]
        inv_z = 1.0 / z
        for i in range(n_blk):
            th = jnp.zeros((SUBLANES, tt), F32)
            for a in range(k16):
                is_a = (ranks[0][i] == float(a)) if tie_break else (scores[0][i] == r1[a])
                th = jnp.where(is_a, counts[a], th)
            rows = slice(SUBLANES * i, SUBLANES * (i + 1))
            th_ref[hd, rows, :] = th
            e1_ref[hd, rows, :] = jnp.exp(jnp.minimum(scores[0][i] - r1[0], 0.0))
        pack = BF16_SUBLANES // SUBLANES
        for i in range(n_blk // pack):
            rows = slice(BF16_SUBLANES * i, BF16_SUBLANES * (i + 1))
            rk_ref[hd, rows, :] = jnp.concatenate(ranks[1][pack * i:pack * (i + 1)], axis=0).astype(rk_ref.dtype)
            e2 = [jnp.exp(jnp.minimum(sc - r2[0], 0.0)) * inv_z for sc in scores[1][pack * i:pack * (i + 1)]]
            e2_ref[hd, rows, :] = jnp.concatenate(e2, axis=0).astype(e2_ref.dtype)
        return bad

    def head(hd, carry):
        route_head(hd, tie_break=True)
        return carry

    lax.fori_loop(0, PEER_HEADS, head, 0)


def peer_route(qp, keys):
    t_all, qw = qp.shape
    tt = _tile(t_all, (512, 256, 128))
    side = lambda dt: jax.ShapeDtypeStruct((PEER_HEADS, N_KEYS, t_all), dt)
    side_spec = pl.BlockSpec((PEER_HEADS, N_KEYS, tt), lambda i: (0, 0, i))
    return pl.pallas_call(
        functools.partial(_peer_route_body, tt=tt),
        grid=(t_all // tt,),
        in_specs=[
            pl.BlockSpec((tt, qw), lambda i: (i, 0)),
            pl.BlockSpec(keys.shape, lambda i: (0, 0, 0, 0)),
        ],
        out_specs=[side_spec] * 4,
        out_shape=[side(F32), side(F32), side(BF16), side(BF16)],
        compiler_params=_cparams(("arbitrary",)),
        name="peer_route",
    )(qp, keys)


def _peer_dense_body(h_ref, xn_ref, th_ref, e1_ref, rk_ref, e2_ref, u_ref, vt_ref, m5_ref,
                     o_ref, acc_scr, a_scr, *, ec):
    j = pl.program_id(1)

    @pl.when(j == 0)
    def _():
        acc_scr[...] = jnp.zeros(acc_scr.shape, F32)

    gdt = rk_ref.dtype
    tt = xn_ref.shape[0]
    hid = _dot_nt(u_ref[...], xn_ref[...])
    act = jax.nn.gelu(hid, approximate=True).astype(gdt)
    rb = BF16_SUBLANES
    for ii in range(ec // N_KEYS):
        i = j * (ec // N_KEYS) + ii
        th_rows = [th_ref[hd, pl.ds(i, 1), :] for hd in range(PEER_HEADS)]
        e1_rows = [e1_ref[hd, pl.ds(i, 1), :] for hd in range(PEER_HEADS)]
        for lt in range(tt // LANES):
            lanes = slice(lt * LANES, (lt + 1) * LANES)
            th = [jnp.broadcast_to(th_rows[hd][:, lanes], (rb, LANES)).astype(gdt) for hd in range(PEER_HEADS)]
            e1 = [jnp.broadcast_to(e1_rows[hd][:, lanes], (rb, LANES)).astype(gdt) for hd in range(PEER_HEADS)]
            for r in range(N_KEYS // rb):
                rows = slice(r * rb, (r + 1) * rb)
                gate = None
                for hd in range(PEER_HEADS):
                    term = jnp.where(rk_ref[hd, rows, lanes] < th[hd], e2_ref[hd, rows, lanes], 0.0) * e1[hd]
                    gate = term if gate is None else gate + term
                out_rows = slice(ii * N_KEYS + r * rb, ii * N_KEYS + (r + 1) * rb)
                a_scr[out_rows, lanes] = (gate * act[out_rows, lanes]).astype(a_scr.dtype)
    acc_scr[...] += _dot(vt_ref[...], a_scr[...])

    @pl.when(j == pl.num_programs(1) - 1)
    def _():
        o_ref[...] = h_ref[...] + m5_ref[0, 0] * acc_scr[...].T


def peer_dense(h2, seq_len, xn, th, e1, rk, e2, u, vt, mod):
    t_all, d = h2.shape
    n_exp = u.shape[0]
    tt = _tile(seq_len, (512, 256, 128))
    ec = 1024
    side_spec = pl.BlockSpec((PEER_HEADS, N_KEYS, tt), lambda i, j: (0, 0, i))
    return pl.pallas_call(
        functools.partial(_peer_dense_body, ec=ec),
        grid=(t_all // tt, n_exp // ec),
        in_specs=[
            pl.BlockSpec((tt, d), lambda i, j: (i, 0)),
            pl.BlockSpec((tt, d), lambda i, j: (i, 0)),
            side_spec, side_spec, side_spec, side_spec,
            pl.BlockSpec((ec, d), lambda i, j: (j, 0)),
            pl.BlockSpec((d, ec), lambda i, j: (0, j)),
            pl.BlockSpec((1, 1, 1, d), lambda i, j: ((i * tt) // seq_len, 5, 0, 0)),
        ],
        out_specs=pl.BlockSpec((tt, d), lambda i, j: (i, 0)),
        out_shape=jax.ShapeDtypeStruct((t_all, d), F32),
        scratch_shapes=[pltpu.VMEM((d, tt), F32), pltpu.VMEM((ec, tt), BF16)],
        compiler_params=_cparams(("arbitrary", "arbitrary")),
        name="peer_dense",
    )(h2, xn, th, e1, rk, e2, u, vt, mod)


def _rope_tables(n):
    n_rows = n // GRID_W
    rows = jnp.repeat(jnp.arange(n_rows, dtype=F32), GRID_W)
    cols = jnp.tile(jnp.arange(GRID_W, dtype=F32), n_rows)
    n_freq = ROPE_HALF // 2
    inv = ROPE_BASE ** (-jnp.arange(n_freq, dtype=F32) / n_freq)
    ang = jnp.concatenate([rows[:, None] * inv, cols[:, None] * inv], axis=-1)
    cos, sin = jnp.cos(ang), jnp.sin(ang)
    reps = LANES // HEAD_DIM
    cos_t = jnp.tile(jnp.concatenate([cos, cos], axis=-1), (1, reps))
    sin_t = jnp.tile(jnp.concatenate([-sin, sin], axis=-1), (1, reps))
    return cos_t, sin_t


def kernel(x, c, ctx, c_ctx, norm1_g, norm2_g, w_mod, b_mod, w_in, q_norm_g, k_norm_g, lam_vecs,
           attn_norm_g, pool_w, pool_scale, conv_w, conv_out_w, w_out, peer_wq, peer_keys, peer_u, peer_v):
    b, s, d = x.shape
    n_ctx = ctx.shape[1]
    depth = w_in.shape[0]
    qk_w = N_HEADS * 2 * HEAD_DIM
    assert b + 1 <= SUBLANES

    cc = jnp.zeros((SUBLANES, d), F32).at[:b].set(c).at[b].set(c_ctx)
    mods = modulation(cc, w_mod, b_mod).reshape(depth, SUBLANES, N_MOD, 1, d)

    cos_t, sin_t = _rope_tables(s)
    chunk = jnp.arange(qk_w) // HEAD_DIM
    block_diag = (chunk[:, None] == chunk[None, :]).astype(BF16)
    tile_gain = lambda g: jnp.tile(g, qk_w // HEAD_DIM).reshape(1, qk_w)

    w_in_b = w_in.astype(BF16)
    pool_w_b = pool_w.astype(BF16)
    conv_out_b = conv_out_w.astype(BF16)
    w_out_b = w_out.astype(BF16)
    peer_wq_b = peer_wq.astype(BF16)
    keys_b = peer_keys.astype(BF16)
    peer_u_b = peer_u.astype(BF16)
    peer_vt_b = jnp.swapaxes(peer_v, 1, 2).astype(BF16)

    def peer_block(hh, seq_len, mod, l):
        h2 = hh.reshape(-1, d)
        qp, xn = norm_matmul(h2, seq_len, norm2_g[l], mod, 3, 4, peer_wq_b[l], F32, emit_xn=True)
        th, e1, rk, e2 = peer_route(qp, keys_b[l])
        out = peer_dense(h2, seq_len, xn, th, e1, rk, e2, peer_u_b[l], peer_vt_b[l], mod)
        return out.reshape(hh.shape)

    def mixer_block(hh, attn, p3, mod, l):
        return mixer(hh, attn, p3, pool_w_b[l], pool_scale[l], conv_w[l], conv_out_b[l], w_out_b[l], mod)

    h, hc = x, ctx
    for l in range(depth):
        last = l == depth - 1
        lam_init = 0.8 - 0.6 * math.exp(-0.3 * l)
        mod = mods[l, :b]
        mod_c = jnp.broadcast_to(mods[l, b:b + 1], (b, N_MOD, 1, d))
        qg, kg = tile_gain(q_norm_g[l]), tile_gain(k_norm_g[l])

        p = norm_matmul(h.reshape(-1, d), s, norm1_g[l], mod, 0, 1, w_in_b[l], F32).reshape(b, s, -1)
        pc = norm_matmul(hc.reshape(-1, d), n_ctx, norm1_g[l], mod_c, 0, 1, w_in_b[l], F32).reshape(b, n_ctx, -1)
        q, k, vt = qkv_post(p, cos_t, sin_t, qg, kg, block_diag, rope=True)
        qc, kc, vtc = qkv_post(pc, cos_t, sin_t, qg, kg, block_diag, rope=False)
        k_all = jnp.concatenate([k, kc], axis=1)
        vt_all = jnp.concatenate([vt, vtc], axis=2)
        score_bound = (HEAD_DIM * Q_SCALE * SCORE_BOUND_SLACK
                       * jnp.max(jnp.abs(q_norm_g[l])) * jnp.max(jnp.abs(k_norm_g[l])))
        attn = attention(q, k_all, vt_all, lam_vecs[l], attn_norm_g[l], lam_init, score_bound)
        h = mixer_block(h, attn, p, mod, l)
        if not last:
            attn_c = attention(qc, kc, vtc, lam_vecs[l], attn_norm_g[l], lam_init, score_bound)
            hc = mixer_block(hc, attn_c, pc, mod_c, l)
        h = peer_block(h, s, mod, l)
        if not last:
            hc = peer_block(hc, n_ctx, mod_c, l)
    return h
```

```python
import functools
import math

import jax
import jax.numpy as jnp
from jax import lax
from jax.experimental import pallas as pl
from jax.experimental.pallas import tpu as pltpu

F32 = jnp.float32
BF16 = jnp.bfloat16

LANES = 128
SUBLANES = 8
BF16_SUBLANES = 16
VMEM_LIMIT_BYTES = 56 * 1024 * 1024

EPS = 1e-6
N_HEADS = 8
HEAD_DIM = 64
V_DIM = 2 * HEAD_DIM
ROPE_HALF = HEAD_DIM // 2
ROPE_BASE = 10000.0
GRID_W = 64
POOL_SIZES = (2, 4, 8, 16)
POOL_HALO = 8
N_MOD = 6
N_KEYS = 128
PEER_HEADS = 8
PEER_TOPK = 16
NEG_BIG = -1e30
LOG2_E = math.log2(math.e)
Q_SCALE = HEAD_DIM ** -0.5 * LOG2_E
MAX_FIXED_SCORE_BOUND = 60.0
SCORE_BOUND_SLACK = 1.01
ATTN_KV_UNROLL = 16


def _cparams(semantics):
    return pltpu.CompilerParams(dimension_semantics=semantics, vmem_limit_bytes=VMEM_LIMIT_BYTES)


def _tile(n, prefs):
    for t in prefs:
        if n % t == 0:
            return t
    raise ValueError(f"no tile among {prefs} divides {n}")


def _dot(a, b):
    return jnp.dot(a, b, preferred_element_type=F32)


def _dot_nt(a, b):
    return lax.dot_general(a, b, (((1,), (1,)), ((), ())), preferred_element_type=F32)


def _mod_body(c_ref, w_ref, b_ref, o_ref):
    c = c_ref[...]
    s = c * jax.nn.sigmoid(c)
    o_ref[0] = _dot(s.astype(BF16), w_ref[0].astype(BF16)) + b_ref[0]


def modulation(cc, w_mod, b_mod):
    depth, d, n = w_mod.shape
    tn = _tile(n, (1536, 1024, 512))
    return pl.pallas_call(
        _mod_body,
        grid=(depth, n // tn),
        in_specs=[
            pl.BlockSpec((SUBLANES, d), lambda l, j: (0, 0)),
            pl.BlockSpec((1, d, tn), lambda l, j: (l, 0, j)),
            pl.BlockSpec((1, 1, tn), lambda l, j: (l, 0, j)),
        ],
        out_specs=pl.BlockSpec((1, SUBLANES, tn), lambda l, j: (l, 0, j)),
        out_shape=jax.ShapeDtypeStruct((depth, SUBLANES, n), F32),
        compiler_params=_cparams(("arbitrary", "arbitrary")),
        name="modulation",
    )(cc, w_mod, b_mod.reshape(depth, 1, n))


def _norm_matmul_body(h_ref, g_ref, sh_ref, sc_ref, w_ref, o_ref, *rest, emit_xn):
    xn_scr = rest[-1]

    @pl.when(pl.program_id(1) == 0)
    def _():
        x = h_ref[...]
        ms = jnp.mean(x * x, axis=-1, keepdims=True)
        y = x * lax.rsqrt(ms + EPS) * g_ref[...]
        xn = y * (1.0 + sc_ref[0, 0]) + sh_ref[0, 0]
        xn_scr[...] = xn.astype(BF16)
        if emit_xn:
            rest[0][...] = xn_scr[...]

    o_ref[...] = _dot(xn_scr[...], w_ref[...]).astype(o_ref.dtype)


def norm_matmul(h2, seq_len, gain, mod, shift_idx, scale_idx, w, out_dtype, emit_xn=False):
    t_all, d = h2.shape
    n = w.shape[1]
    tt = _tile(seq_len, (512, 256, 128))
    tn = _tile(n, (2048, 1024))
    out_shape = [jax.ShapeDtypeStruct((t_all, n), out_dtype)]
    out_specs = [pl.BlockSpec((tt, tn), lambda i, j: (i, j))]
    if emit_xn:
        out_shape.append(jax.ShapeDtypeStruct((t_all, d), BF16))
        out_specs.append(pl.BlockSpec((tt, d), lambda i, j: (i, 0)))
    res = pl.pallas_call(
        functools.partial(_norm_matmul_body, emit_xn=emit_xn),
        grid=(t_all // tt, n // tn),
        in_specs=[
            pl.BlockSpec((tt, d), lambda i, j: (i, 0)),
            pl.BlockSpec((1, d), lambda i, j: (0, 0)),
            pl.BlockSpec((1, 1, 1, d), lambda i, j: ((i * tt) // seq_len, shift_idx, 0, 0)),
            pl.BlockSpec((1, 1, 1, d), lambda i, j: ((i * tt) // seq_len, scale_idx, 0, 0)),
            pl.BlockSpec((d, tn), lambda i, j: (0, j)),
        ],
        out_specs=out_specs,
        out_shape=out_shape,
        scratch_shapes=[pltpu.VMEM((tt, d), BF16)],
        compiler_params=_cparams(("arbitrary", "arbitrary")),
        name="norm_matmul",
    )(h2, gain.reshape(1, d), mod, mod, w)
    return res if emit_xn else res[0]


def _qkv_post_body(pq_ref, pk_ref, pv_ref, cos_ref, sin_ref, qg_ref, kg_ref, bd_ref,
                   q_ref, k_ref, vt_ref, *, rope):
    lane = lax.broadcasted_iota(jnp.int32, (pq_ref.shape[1], LANES), 1)
    first_half = (lane % HEAD_DIM) < ROPE_HALF

    def norm_rope(x, g, scale):
        ms = _dot((x * x).astype(BF16), bd_ref[...]) * (1.0 / HEAD_DIM)
        y = x * lax.rsqrt(ms + EPS) * g
        if scale != 1.0:
            y = y * scale
        if not rope:
            return y
        cos = cos_ref[...]
        sin = sin_ref[...]
        parts = []
        for hd in range(y.shape[1] // LANES):
            yh = y[:, hd * LANES:(hd + 1) * LANES]
            partner = jnp.where(first_half, pltpu.roll(yh, LANES - ROPE_HALF, 1), pltpu.roll(yh, ROPE_HALF, 1))
            parts.append(yh * cos + partner * sin)
        return jnp.concatenate(parts, axis=1)

    q_ref[0] = norm_rope(pq_ref[0].astype(F32), qg_ref[...], Q_SCALE).astype(BF16)
    k_ref[0] = norm_rope(pk_ref[0].astype(F32), kg_ref[...], 1.0).astype(BF16)
    vt_ref[0] = pv_ref[0].astype(F32).T.astype(BF16)


def qkv_post(p3, cos_t, sin_t, q_gain, k_gain, block_diag, rope):
    b, l, _ = p3.shape
    w = N_HEADS * V_DIM
    tt = _tile(l, (256, 128))
    col = lambda c: pl.BlockSpec((1, tt, w), lambda i, j, c=c: (i, j, c))
    return pl.pallas_call(
        functools.partial(_qkv_post_body, rope=rope),
        grid=(b, l // tt),
        in_specs=[
            col(0), col(1), col(2),
            pl.BlockSpec((tt, LANES), lambda i, j: (j, 0)),
            pl.BlockSpec((tt, LANES), lambda i, j: (j, 0)),
            pl.BlockSpec((1, w), lambda i, j: (0, 0)),
            pl.BlockSpec((1, w), lambda i, j: (0, 0)),
            pl.BlockSpec((w, w), lambda i, j: (0, 0)),
        ],
        out_specs=[
            pl.BlockSpec((1, tt, w), lambda i, j: (i, j, 0)),
            pl.BlockSpec((1, tt, w), lambda i, j: (i, j, 0)),
            pl.BlockSpec((1, w, tt), lambda i, j: (i, 0, j)),
        ],
        out_shape=[
            jax.ShapeDtypeStruct((b, l, w), BF16),
            jax.ShapeDtypeStruct((b, l, w), BF16),
            jax.ShapeDtypeStruct((b, w, l), BF16),
        ],
        compiler_params=_cparams(("arbitrary", "arbitrary")),
        name="qkv_post",
    )(p3, p3, p3, cos_t, sin_t, q_gain, k_gain, block_diag)


def _attention_body(mb_ref, lv_ref, g_ref, q_ref, k_ref, vt_ref, o_ref, m_scr, l_scr, acc_scr,
                    *, tq, tk, lk, lam_init, fixed_max, unroll):
    lane = lax.broadcasted_iota(jnp.int32, (tq, LANES), 1)
    q = q_ref[0].astype(F32)
    q2 = jnp.concatenate([jnp.where(lane < HEAD_DIM, q, 0.0), jnp.where(lane >= HEAD_DIM, q, 0.0)], axis=0)
    q2 = q2.astype(BF16)
    n = 2 * tq

    if not fixed_max:
        m_scr[...] = jnp.full(m_scr.shape, NEG_BIG, F32)
    l_scr[...] = jnp.zeros(l_scr.shape, F32)
    acc_scr[...] = jnp.zeros(acc_scr.shape, F32)

    def block(start, size):
        kb = k_ref[0, pl.ds(start, size), :]
        vb = vt_ref[0, :, pl.ds(start, size)]
        s = _dot_nt(kb, q2)
        if fixed_max:
            p = jnp.exp2(s - mb_ref[...])
            l_scr[...] += jnp.sum(p.reshape(size // SUBLANES, SUBLANES, n), axis=0)
            acc_scr[...] += _dot(vb, p.astype(BF16))
        else:
            m_old = m_scr[...]
            m_new = jnp.maximum(m_old, jnp.max(s, axis=0, keepdims=True))
            alpha = jnp.exp2(m_old - m_new)
            p = jnp.exp2(s - m_new)
            l_scr[...] = alpha * l_scr[...] + jnp.sum(p.reshape(size // SUBLANES, SUBLANES, n), axis=0)
            m_scr[...] = m_new
            acc_scr[...] = acc_scr[...] * alpha + _dot(vb, p.astype(BF16))

    n_full = lk // tk
    rem = lk - n_full * tk
    if n_full > 0:
        def step(j, carry):
            block(pl.multiple_of(j * tk, tk), tk)
            return carry
        lax.fori_loop(0, n_full, step, 0, unroll=math.gcd(unroll, n_full))
    if rem > 0:
        block(n_full * tk, rem)

    lv = lv_ref[...]
    lam = (jnp.exp(jnp.sum(lv[0:1] * lv[1:2], axis=1, keepdims=True))
           - jnp.exp(jnp.sum(lv[2:3] * lv[3:4], axis=1, keepdims=True)) + lam_init)
    acc = acc_scr[...]
    l = jnp.sum(l_scr[...], axis=0, keepdims=True)
    o = acc[:, :tq] / l[:, :tq] - lam * (acc[:, tq:] / l[:, tq:])
    ms = jnp.mean(o * o, axis=0, keepdims=True)
    y = o * lax.rsqrt(ms + EPS) * g_ref[...] * (1.0 - lam_init)
    o_ref[0] = y.T


def attention(q, k, vt, lam_vecs, out_gain, lam_init, score_bound):
    b, lq, w = q.shape
    lk = k.shape[1]
    tq = _tile(lq, (256, 128))
    tk = min(512, lk)

    def call(fixed_max):
        return _attention_call(q, k, vt, lam_vecs, out_gain, score_bound, b, lq, lk, w, tq, tk, lam_init, fixed_max)

    return lax.cond(score_bound <= MAX_FIXED_SCORE_BOUND, lambda: call(True), lambda: call(False))


def _attention_call(q, k, vt, lam_vecs, out_gain, score_bound, b, lq, lk, w, tq, tk, lam_init, fixed_max):
    return pl.pallas_call(
        functools.partial(_attention_body, tq=tq, tk=tk, lk=lk, lam_init=lam_init, fixed_max=fixed_max,
                          unroll=ATTN_KV_UNROLL),
        grid=(b, N_HEADS, lq // tq),
        in_specs=[
            pl.BlockSpec((1, 1), lambda i, h, j: (0, 0)),
            pl.BlockSpec((4, HEAD_DIM), lambda i, h, j: (0, 0)),
            pl.BlockSpec((V_DIM, 1), lambda i, h, j: (0, 0)),
            pl.BlockSpec((1, tq, V_DIM), lambda i, h, j: (i, j, h)),
            pl.BlockSpec((1, lk, V_DIM), lambda i, h, j: (i, 0, h)),
            pl.BlockSpec((1, V_DIM, lk), lambda i, h, j: (i, h, 0)),
        ],
        out_specs=pl.BlockSpec((1, tq, V_DIM), lambda i, h, j: (i, j, h)),
        out_shape=jax.ShapeDtypeStruct((b, lq, w), F32),
        scratch_shapes=[
            pltpu.VMEM((1, 2 * tq), F32),
            pltpu.VMEM((SUBLANES, 2 * tq), F32),
            pltpu.VMEM((V_DIM, 2 * tq), F32),
        ],
        compiler_params=_cparams(("arbitrary", "arbitrary", "arbitrary")),
        name="attention_fixed" if fixed_max else "attention_online",
    )(score_bound.reshape(1, 1).astype(F32), lam_vecs, out_gain.reshape(V_DIM, 1), q, k, vt)


def _mixer_body(h_ref, a_ref, u_ref, up_ref, un_ref, ch_ref, chp_ref, chn_ref, cb_ref,
                cc_ref, ccp_ref, ccn_ref, g0_ref, g1_ref, g2_ref,
                pw_ref, ps_ref, cw_ref, cow_ref, wo_ref, m2_ref, o_ref, *, tt, seq_len):
    t0 = pl.program_id(1) * tt
    rows = tt + 2 * POOL_HALO
    d = h_ref.shape[2]
    pos = lax.broadcasted_iota(jnp.int32, (rows, d), 0) + (t0 - POOL_HALO)
    in_seq = (pos >= 0) & (pos < seq_len)

    def extended(prev_ref, cur_ref, next_ref):
        e = jnp.concatenate([prev_ref[0].astype(F32), cur_ref[0].astype(F32), next_ref[0].astype(F32)], axis=0)
        return jnp.where(in_seq, e, 0.0)

    ue = extended(up_ref, u_ref, un_ref)
    u = ue[POOL_HALO:POOL_HALO + tt]
    gw = ue.shape[1] // len(POOL_SIZES)
    tpos = lax.broadcasted_iota(jnp.int32, (tt, gw), 0) + t0
    pooled = []
    for gi, wsz in enumerate(POOL_SIZES):
        lo = wsz // 2
        hi = wsz - 1 - lo
        e = ue[:, gi * gw:(gi + 1) * gw]
        acc = e
        span = 1
        while span < wsz:
            acc = acc + pltpu.roll(acc, rows - span, 0)
            span *= 2
        win = pltpu.roll(acc, lo, 0)[POOL_HALO:POOL_HALO + tt]
        cnt = (jnp.minimum(tpos + hi + 1, seq_len) - jnp.maximum(tpos - lo, 0)).astype(F32)
        pooled.append(win / cnt - u[:, gi * gw:(gi + 1) * gw])
    pool = jnp.concatenate(
        [_dot(pooled[gi].astype(BF16), pw_ref[gi]) for gi in range(len(POOL_SIZES))], axis=1) * ps_ref[...]

    che = extended(chp_ref, ch_ref, chn_ref) * extended(ccp_ref, cc_ref, ccn_ref)
    cw = cw_ref[...]
    conv = (cw[0:1] * pltpu.roll(che, 1, 0)[POOL_HALO:POOL_HALO + tt]
            + cw[1:2] * che[POOL_HALO:POOL_HALO + tt]
            + cw[2:3] * pltpu.roll(che, rows - 1, 0)[POOL_HALO:POOL_HALO + tt])
    conv = _dot((cb_ref[0].astype(F32) * conv).astype(BF16), cow_ref[...])

    merged = (jax.nn.sigmoid(g0_ref[0].astype(F32)) * a_ref[0]
              + jax.nn.sigmoid(g1_ref[0].astype(F32)) * pool
              + jax.nn.sigmoid(g2_ref[0].astype(F32)) * conv)
    mix = _dot(merged.astype(BF16), wo_ref[...])
    o_ref[0] = h_ref[0] + m2_ref[0, 0] * mix


def mixer(h3, attn, p3, pool_w, pool_scale, conv_w, conv_out_w, w_out, mod):
    b, l, d = h3.shape
    tt = _tile(l, (256, 128))
    hb = tt // POOL_HALO
    n_hb = l // POOL_HALO
    cur = lambda c: pl.BlockSpec((1, tt, d), lambda i, j, c=c: (i, j, c))
    prev = lambda c: pl.BlockSpec((1, POOL_HALO, d), lambda i, j, c=c: (i, jnp.maximum(j * hb - 1, 0), c))
    nxt = lambda c: pl.BlockSpec((1, POOL_HALO, d), lambda i, j, c=c: (i, jnp.minimum((j + 1) * hb, n_hb - 1), c))
    full = lambda shape: pl.BlockSpec(shape, lambda i, j: (0,) * len(shape))
    return pl.pallas_call(
        functools.partial(_mixer_body, tt=tt, seq_len=l),
        grid=(b, l // tt),
        in_specs=[
            cur(0), cur(0),
            cur(3), prev(3), nxt(3),
            cur(4), prev(4), nxt(4), cur(5),
            cur(6), prev(6), nxt(6),
            cur(7), cur(8), cur(9),
            full(pool_w.shape), full((1, d)), full(conv_w.shape), full((d, d)), full((d, d)),
            pl.BlockSpec((1, 1, 1, d), lambda i, j: (i, 2, 0, 0)),
        ],
        out_specs=pl.BlockSpec((1, tt, d), lambda i, j: (i, j, 0)),
        out_shape=jax.ShapeDtypeStruct((b, l, d), F32),
        compiler_params=_cparams(("arbitrary", "arbitrary")),
        name="mixer",
    )(h3, attn, p3, p3, p3, p3, p3, p3, p3, p3, p3, p3, p3, p3, p3,
      pool_w, pool_scale.reshape(1, d), conv_w, conv_out_w, w_out, mod)


def _sublane_all(op, x):
    for shift in (4, 2, 1):
        x = op(x, pltpu.roll(x, shift, 0))
    return x


def _tree(op, xs):
    xs = list(xs)
    while len(xs) > 1:
        xs = [op(xs[i], xs[i + 1]) if i + 1 < len(xs) else xs[i] for i in range(0, len(xs), 2)]
    return xs[0]


def _extract_top(blocks, index_blocks, n_take, tie_break, want_order=True):
    assert tie_break and want_order
    work = list(blocks)
    order = [jnp.full(b.shape, float(n_take), F32) for b in blocks]
    taken = []
    big = float(1 << 20)
    for a in range(n_take):
        m = work[0]
        for w in work[1:]:
            m = jnp.maximum(m, w)
        m = jnp.max(m, axis=0, keepdims=True)
        cand = [jnp.where(w == m, ib, big) for w, ib in zip(work, index_blocks)]
        first = cand[0]
        for c in cand[1:]:
            first = jnp.minimum(first, c)
        first = jnp.min(first, axis=0, keepdims=True)
        hit = [ib == first for ib in index_blocks]
        work = [jnp.where(ht, NEG_BIG * 2.0, w) for ht, w in zip(hit, work)]
        order = [jnp.where(ht, float(a), o) for ht, o in zip(hit, order)]
        taken.append(m)
    return taken, order, None


def _peer_route_body(q_ref, keys_ref, th_ref, e1_ref, rk_ref, e2_ref, *, tt):
    row = lax.broadcasted_iota(jnp.int32, (SUBLANES, tt), 0).astype(F32)
    n_blk = N_KEYS // SUBLANES
    key_index = [row + float(SUBLANES * i) for i in range(n_blk)]
    k16 = PEER_TOPK
    half_blocks = k16 // SUBLANES

    def stack_rows(vals):
        out = jnp.zeros((SUBLANES, tt), F32)
        for i, v in enumerate(vals):
            out = jnp.where(row == float(i), v, out)
        return out

    def route_head(hd, tie_break):
        tops, ranks, scores, bad = [], [], [], None
        for half in range(2):
            c0 = pl.multiple_of((hd * 2 + half) * N_KEYS, N_KEYS)
            s = _dot_nt(keys_ref[hd, half], q_ref[:, pl.ds(c0, N_KEYS)].astype(BF16))
            blocks = [s[SUBLANES * i:SUBLANES * (i + 1)] for i in range(n_blk)]
            taken, order, count = _extract_top(blocks, key_index, k16, tie_break,
                                               want_order=tie_break or half == 1)
            tops.append(taken)
            ranks.append(order)
            scores.append(blocks)
            if not tie_break:
                bad = count if bad is None else jnp.maximum(bad, count)
        r1, r2 = tops
        r1_hi = stack_rows(r1[SUBLANES:])
        r2_blk = [stack_rows(r2[SUBLANES * i:SUBLANES * (i + 1)]) for i in range(half_blocks)]
        cblocks, cindex, cvalid = [], [], []
        for a in range(SUBLANES):
            nb = k16 // (a + 1)
            for bi in range(half_blocks):
                if bi * SUBLANES >= nb:
                    continue
                valid = row < float(nb - bi * SUBLANES)
                cblocks.append(jnp.where(valid, r1[a] + r2_blk[bi], NEG_BIG))
                cindex.append(row + float(a * k16 + bi * SUBLANES))
                cvalid.append((a, bi))
        cblocks.append(r1_hi + r2[0])
        cindex.append((row + float(SUBLANES)) * float(k16))
        _, corder, ccount = _extract_top(cblocks, cindex, k16, tie_break)
        if not tie_break:
            bad = jnp.maximum(bad, ccount)
        sel = [(o < float(k16)).astype(F32) for o in corder]
        cmax = r1[0] + r2[0]
        z = None
        for cb, sl in zip(cblocks, sel):
            part = jnp.sum(sl * jnp.exp(cb - cmax), axis=0, keepdims=True)
            z = part if z is None else z + part
        counts = []
        for a in range(SUBLANES):
            na = None
            for (ca, _), sl in zip(cvalid, sel[:-1]):
                if ca == a:
                    part = jnp.sum(sl, axis=0, keepdims=True)
                    na = part if na is None else na + part
            counts.append(na)
        tail = sel[-1]
        counts += [tail[i:i + 1] for i in range(SUBLANES)]
        inv_z = 1.0 / z
        for i in range(n_blk):
            th = jnp.zeros((SUBLANES, tt), F32)
            for a in range(k16):
                is_a = (ranks[0][i] == float(a)) if tie_break else (scores[0][i] == r1[a])
                th = jnp.where(is_a, counts[a], th)
            rows = slice(SUBLANES * i, SUBLANES * (i + 1))
            th_ref[hd, rows, :] = th
            e1_ref[hd, rows, :] = jnp.exp(jnp.minimum(scores[0][i] - r1[0], 0.0))
        pack = BF16_SUBLANES // SUBLANES
        for i in range(n_blk // pack):
            rows = slice(BF16_SUBLANES * i, BF16_SUBLANES * (i + 1))
            rk_ref[hd, rows, :] = jnp.concatenate(ranks[1][pack * i:pack * (i + 1)], axis=0).astype(rk_ref.dtype)
            e2 = [jnp.exp(jnp.minimum(sc - r2[0], 0.0)) * inv_z for sc in scores[1][pack * i:pack * (i + 1)]]
            e2_ref[hd, rows, :] = jnp.concatenate(e2, axis=0).astype(e2_ref.dtype)
        return bad

    def head(hd, carry):
        route_head(hd, tie_break=True)
        return carry

    lax.fori_loop(0, PEER_HEADS, head, 0)


def peer_route(qp, keys):
    t_all, qw = qp.shape
    tt = _tile(t_all, (512, 256, 128))
    side = lambda dt: jax.ShapeDtypeStruct((PEER_HEADS, N_KEYS, t_all), dt)
    side_spec = pl.BlockSpec((PEER_HEADS, N_KEYS, tt), lambda i: (0, 0, i))
    return pl.pallas_call(
        functools.partial(_peer_route_body, tt=tt),
        grid=(t_all // tt,),
        in_specs=[
            pl.BlockSpec((tt, qw), lambda i: (i, 0)),
            pl.BlockSpec(keys.shape, lambda i: (0, 0, 0, 0)),
        ],
        out_specs=[side_spec] * 4,
        out_shape=[side(F32), side(F32), side(BF16), side(BF16)],
        compiler_params=_cparams(("arbitrary",)),
        name="peer_route",
    )(qp, keys)


def _peer_dense_body(h_ref, xn_ref, th_ref, e1_ref, rk_ref, e2_ref, u_ref, vt_ref, m5_ref,
                     o_ref, acc_scr, a_scr, *, ec):
    j = pl.program_id(1)

    @pl.when(j == 0)
    def _():
        acc_scr[...] = jnp.zeros(acc_scr.shape, F32)

    gdt = rk_ref.dtype
    tt = xn_ref.shape[0]
    hid = _dot_nt(u_ref[...], xn_ref[...])
    act = jax.nn.gelu(hid, approximate=True).astype(gdt)
    rb = BF16_SUBLANES
    for ii in range(ec // N_KEYS):
        i = j * (ec // N_KEYS) + ii
        th_rows = [th_ref[hd, pl.ds(i, 1), :] for hd in range(PEER_HEADS)]
        e1_rows = [e1_ref[hd, pl.ds(i, 1), :] for hd in range(PEER_HEADS)]
        for lt in range(tt // LANES):
            lanes = slice(lt * LANES, (lt + 1) * LANES)
            th = [jnp.broadcast_to(th_rows[hd][:, lanes], (rb, LANES)).astype(gdt) for hd in range(PEER_HEADS)]
            e1 = [jnp.broadcast_to(e1_rows[hd][:, lanes], (rb, LANES)).astype(gdt) for hd in range(PEER_HEADS)]
            for r in range(N_KEYS // rb):
                rows = slice(r * rb, (r + 1) * rb)
                gate = None
                for hd in range(PEER_HEADS):
                    term = jnp.where(rk_ref[hd, rows, lanes] < th[hd], e2_ref[hd, rows, lanes], 0.0) * e1[hd]
                    gate = term if gate is None else gate + term
                out_rows = slice(ii * N_KEYS + r * rb, ii * N_KEYS + (r + 1) * rb)
                a_scr[out_rows, lanes] = (gate * act[out_rows, lanes]).astype(a_scr.dtype)
    acc_scr[...] += _dot(vt_ref[...], a_scr[...])

    @pl.when(j == pl.num_programs(1) - 1)
    def _():
        o_ref[...] = h_ref[...] + m5_ref[0, 0] * acc_scr[...].T


def peer_dense(h2, seq_len, xn, th, e1, rk, e2, u, vt, mod):
    t_all, d = h2.shape
    n_exp = u.shape[0]
    tt = _tile(seq_len, (512, 256, 128))
    ec = 1024
    side_spec = pl.BlockSpec((PEER_HEADS, N_KEYS, tt), lambda i, j: (0, 0, i))
    return pl.pallas_call(
        functools.partial(_peer_dense_body, ec=ec),
        grid=(t_all // tt, n_exp // ec),
        in_specs=[
            pl.BlockSpec((tt, d), lambda i, j: (i, 0)),
            pl.BlockSpec((tt, d), lambda i, j: (i, 0)),
            side_spec, side_spec, side_spec, side_spec,
            pl.BlockSpec((ec, d), lambda i, j: (j, 0)),
            pl.BlockSpec((d, ec), lambda i, j: (0, j)),
            pl.BlockSpec((1, 1, 1, d), lambda i, j: ((i * tt) // seq_len, 5, 0, 0)),
        ],
        out_specs=pl.BlockSpec((tt, d), lambda i, j: (i, 0)),
        out_shape=jax.ShapeDtypeStruct((t_all, d), F32),
        scratch_shapes=[pltpu.VMEM((d, tt), F32), pltpu.VMEM((ec, tt), BF16)],
        compiler_params=_cparams(("arbitrary", "arbitrary")),
        name="peer_dense",
    )(h2, xn, th, e1, rk, e2, u, vt, mod)


def _rope_tables(n):
    n_rows = n // GRID_W
    rows = jnp.repeat(jnp.arange(n_rows, dtype=F32), GRID_W)
    cols = jnp.tile(jnp.arange(GRID_W, dtype=F32), n_rows)
    n_freq = ROPE_HALF // 2
    inv = ROPE_BASE ** (-jnp.arange(n_freq, dtype=F32) / n_freq)
    ang = jnp.concatenate([rows[:, None] * inv, cols[:, None] * inv], axis=-1)
    cos, sin = jnp.cos(ang), jnp.sin(ang)
    reps = LANES // HEAD_DIM
    cos_t = jnp.tile(jnp.concatenate([cos, cos], axis=-1), (1, reps))
    sin_t = jnp.tile(jnp.concatenate([-sin, sin], axis=-1), (1, reps))
    return cos_t, sin_t


def kernel(x, c, ctx, c_ctx, norm1_g, norm2_g, w_mod, b_mod, w_in, q_norm_g, k_norm_g, lam_vecs,
           attn_norm_g, pool_w, pool_scale, conv_w, conv_out_w, w_out, peer_wq, peer_keys, peer_u, peer_v):
    b, s, d = x.shape
    n_ctx = ctx.shape[1]
    depth = w_in.shape[0]
    qk_w = N_HEADS * 2 * HEAD_DIM
    assert b + 1 <= SUBLANES

    cc = jnp.zeros((SUBLANES, d), F32).at[:b].set(c).at[b].set(c_ctx)
    mods = modulation(cc, w_mod, b_mod).reshape(depth, SUBLANES, N_MOD, 1, d)

    cos_t, sin_t = _rope_tables(s)
    chunk = jnp.arange(qk_w) // HEAD_DIM
    block_diag = (chunk[:, None] == chunk[None, :]).astype(BF16)
    tile_gain = lambda g: jnp.tile(g, qk_w // HEAD_DIM).reshape(1, qk_w)

    w_in_b = w_in.astype(BF16)
    pool_w_b = pool_w.astype(BF16)
    conv_out_b = conv_out_w.astype(BF16)
    w_out_b = w_out.astype(BF16)
    peer_wq_b = peer_wq.astype(BF16)
    keys_b = peer_keys.astype(BF16)
    peer_u_b = peer_u.astype(BF16)
    peer_vt_b = jnp.swapaxes(peer_v, 1, 2).astype(BF16)

    def peer_block(hh, seq_len, mod, l):
        h2 = hh.reshape(-1, d)
        qp, xn = norm_matmul(h2, seq_len, norm2_g[l], mod, 3, 4, peer_wq_b[l], F32, emit_xn=True)
        th, e1, rk, e2 = peer_route(qp, keys_b[l])
        out = peer_dense(h2, seq_len, xn, th, e1, rk, e2, peer_u_b[l], peer_vt_b[l], mod)
        return out.reshape(hh.shape)

    def mixer_block(hh, attn, p3, mod, l):
        return mixer(hh, attn, p3, pool_w_b[l], pool_scale[l], conv_w[l], conv_out_b[l], w_out_b[l], mod)

    h, hc = x, ctx
    for l in range(depth):
        last = l == depth - 1
        lam_init = 0.8 - 0.6 * math.exp(-0.3 * l)
        mod = mods[l, :b]
        mod_c = jnp.broadcast_to(mods[l, b:b + 1], (b, N_MOD, 1, d))
        qg, kg = tile_gain(q_norm_g[l]), tile_gain(k_norm_g[l])

        p = norm_matmul(h.reshape(-1, d), s, norm1_g[l], mod, 0, 1, w_in_b[l], F32).reshape(b, s, -1)
        pc = norm_matmul(hc.reshape(-1, d), n_ctx, norm1_g[l], mod_c, 0, 1, w_in_b[l], F32).reshape(b, n_ctx, -1)
        q, k, vt = qkv_post(p, cos_t, sin_t, qg, kg, block_diag, rope=True)
        qc, kc, vtc = qkv_post(pc, cos_t, sin_t, qg, kg, block_diag, rope=False)
        k_all = jnp.concatenate([k, kc], axis=1)
        vt_all = jnp.concatenate([vt, vtc], axis=2)
        score_bound = (HEAD_DIM * Q_SCALE * SCORE_BOUND_SLACK
                       * jnp.max(jnp.abs(q_norm_g[l])) * jnp.max(jnp.abs(k_norm_g[l])))
        attn = attention(q, k_all, vt_all, lam_vecs[l], attn_norm_g[l], lam_init, score_bound)
        h = mixer_block(h, attn, p, mod, l)
        if not last:
            attn_c = attention(qc, kc, vtc, lam_vecs[l], attn_norm_g[l], lam_init, score_bound)
            hc = mixer_block(hc, attn_c, pc, mod_c, l)
        h = peer_block(h, s, mod, l)
        if not last:
            hc = peer_block(hc, n_ctx, mod_c, l)
    return h
```
